```python
import jax, jax.numpy as jnp
from jax import lax
import numpy as np

D_MODEL = 2048
BATCH = 2
SEQ = 16384
DEPTH = 2

CONV_DIM = 512
CONV_K = 3
MLA_HEADS = 8
QK_NOPE = 128
QK_ROPE = 64
QK_HEAD = QK_NOPE + QK_ROPE
V_DIM = 128
MLA_DIM = MLA_HEADS * V_DIM
Q_LORA = 512
KV_LORA = 256
ROPE_THETA = 10000.0
Q_BLOCK = 128
RWKV_HEADS = 8
RWKV_N = 64
RWKV_DIM = RWKV_HEADS * RWKV_N
DECAY_LORA = 64
A_LORA = 64
RWKV_SHIFT_SIZES = (RWKV_DIM, DECAY_LORA, RWKV_DIM, RWKV_DIM, A_LORA)
RWKV_SHIFT_DIM = 3 * RWKV_DIM + DECAY_LORA + A_LORA
RWKV_GN_EPS = 64e-5
MIX_DIM = CONV_DIM + MLA_DIM + RWKV_DIM
IN_SIZES = (CONV_DIM, CONV_DIM, CONV_DIM, CONV_DIM,
            Q_LORA, KV_LORA, QK_ROPE, MLA_DIM,
            RWKV_SHIFT_DIM, RWKV_DIM)
IN_TOTAL = sum(IN_SIZES)
LN_EPS = 1e-5
RMS_EPS = 1e-6
DEEPNORM_ALPHA = (2 * DEPTH) ** 0.25
DEEPNORM_BETA = (8 * DEPTH) ** -0.25

kernel_name = "hybrid_conv_mla_rwkv7_deepnorm"


def _split(u, sizes):
    idx, acc = [], 0
    for s in sizes[:-1]:
        acc += s
        idx.append(acc)
    return jnp.split(u, idx, axis=-1)


def _layer_norm(x, g, b):
    xf = x.astype(jnp.float32)
    mu = jnp.mean(xf, -1, keepdims=True)
    var = jnp.mean(jnp.square(xf - mu), -1, keepdims=True)
    return ((xf - mu) * lax.rsqrt(var + LN_EPS)).astype(x.dtype) * g + b


def _rms_norm(x, g):
    xf = x.astype(jnp.float32)
    return (xf * lax.rsqrt(jnp.mean(xf * xf, -1, keepdims=True) + RMS_EPS)).astype(x.dtype) * g


def _causal_shift(u, n):
    return jnp.pad(u, ((0, 0), (n, 0), (0, 0)))[:, :u.shape[1]]


def _rope(x, cos, sin):
    x1, x2 = jnp.split(x, 2, axis=-1)
    return jnp.concatenate([x1 * cos - x2 * sin, x2 * cos + x1 * sin], axis=-1)


def _short_conv_branch(b_gate, c_gate, h, gate, conv_w):
    u = c_gate * h
    S = u.shape[1]
    up = jnp.pad(u, ((0, 0), (CONV_K - 1, 0), (0, 0)))
    y = sum(conv_w[j] * up[:, j:j + S] for j in range(CONV_K))
    return b_gate * y * jax.nn.silu(gate)


def _causal_block_attention(q, k, v):
    B, S, H, D = q.shape
    n_blk = S // Q_BLOCK
    scale = D ** -0.5
    q_blocks = q.reshape(B, n_blk, Q_BLOCK, H, D).transpose(1, 0, 2, 3, 4)
    k_pos = jnp.arange(S)

    def one_block(args):
        qb, blk = args
        s = jnp.einsum('bqhd,bkhd->bhqk', qb, k, preferred_element_type=jnp.float32) * scale
        q_pos = blk * Q_BLOCK + jnp.arange(Q_BLOCK)
        s = jnp.where(k_pos[None, :] <= q_pos[:, None], s, -jnp.inf)
        p = jax.nn.softmax(s, axis=-1).astype(v.dtype)
        return jnp.einsum('bhqk,bkhd->bqhd', p, v)

    o = lax.map(one_block, (q_blocks, jnp.arange(n_blk)))
    return o.transpose(1, 0, 2, 3, 4).reshape(B, S, H, v.shape[-1])


def _mla_branch(c_q, c_kv, k_pe, gate, q_norm_g, w_uq, kv_norm_g, w_ukv, cos, sin):
    B, S, _ = c_q.shape
    q = (_rms_norm(c_q, q_norm_g) @ w_uq).reshape(B, S, MLA_HEADS, QK_HEAD)
    q_nope, q_pe = q[..., :QK_NOPE], q[..., QK_NOPE:]
    kv = (_rms_norm(c_kv, kv_norm_g) @ w_ukv).reshape(B, S, MLA_HEADS, QK_NOPE + V_DIM)
    k_nope, v = kv[..., :QK_NOPE], kv[..., QK_NOPE:]
    q = jnp.concatenate([q_nope, _rope(q_pe, cos, sin)], axis=-1)
    k_pe = _rope(k_pe[:, :, None, :], cos, sin)
    k = jnp.concatenate([k_nope, jnp.broadcast_to(k_pe, (B, S, MLA_HEADS, QK_ROPE))], axis=-1)
    o = _causal_block_attention(q, k, v)
    return o.reshape(B, S, MLA_DIM) * jax.nn.silu(gate)


def _rwkv7_scan(r, decay, k, v, a_vec, b_vec):
    B, S, H, N = r.shape
    xs = tuple(t.astype(jnp.float32).transpose(1, 0, 2, 3) for t in (r, decay, k, v, a_vec, b_vec))

    def step(state, inp):
        r_t, w_t, k_t, v_t, a_t, b_t = inp
        sa = jnp.einsum('bhvk,bhk->bhv', state, a_t)
        state = (state * w_t[:, :, None, :] + sa[..., None] * b_t[:, :, None, :]
                 + v_t[..., None] * k_t[:, :, None, :])
        return state, jnp.einsum('bhvk,bhk->bhv', state, r_t)

    s0 = jnp.zeros((B, H, N, N), jnp.float32)
    _, y = lax.scan(step, s0, xs)
    return y.transpose(1, 0, 2, 3).astype(r.dtype)


def _rwkv7_branch(shift_cols, gate, mu, w0, w2, a0, a2, k_k, k_a, r_k, gn_g, gn_b):
    B, S, _ = shift_cols.shape
    xs = shift_cols + (_causal_shift(shift_cols, 1) - shift_cols) * mu
    r, wd, k, v, ad = _split(xs, RWKV_SHIFT_SIZES)
    w = -jax.nn.softplus(-(w0 + jnp.tanh(wd) @ w2)) - 0.5
    decay = jnp.exp(-jnp.exp(w.astype(jnp.float32)))
    a = jax.nn.sigmoid(a0 + ad @ a2)
    kk = (k * k_k).reshape(B, S, RWKV_HEADS, RWKV_N).astype(jnp.float32)
    kk = (kk / jnp.maximum(jnp.linalg.norm(kk, axis=-1, keepdims=True), 1e-12)).astype(k.dtype)
    k = k * (1 + (a - 1) * k_a)
    hs = lambda t: t.reshape(B, S, RWKV_HEADS, RWKV_N)
    r, k, v, a, decay = hs(r), hs(k), hs(v), hs(a), hs(decay)
    y = _rwkv7_scan(r, decay, k, v, -kk, kk * a)
    yf = y.astype(jnp.float32)
    ym = jnp.mean(yf, -1, keepdims=True)
    yv = jnp.mean(jnp.square(yf - ym), -1, keepdims=True)
    y = ((yf - ym) * lax.rsqrt(yv + RWKV_GN_EPS)).astype(v.dtype)
    y = y * gn_g.reshape(RWKV_HEADS, RWKV_N) + gn_b.reshape(RWKV_HEADS, RWKV_N)
    y = y + jnp.sum(r * k * r_k, axis=-1, keepdims=True) * v
    return y.reshape(B, S, RWKV_DIM) * jax.nn.silu(gate)


def setup_inputs(seed: int = 0) -> dict:
    key = jax.random.key(seed)
    ks = jax.random.split(key, 24)
    n = lambda i, shape: jax.random.normal(ks[i], shape, jnp.float32)
    L = DEPTH
    x = n(0, (BATCH, SEQ, D_MODEL))
    positions = (jnp.arange(SEQ, dtype=jnp.int32)[None, :]
                 + jax.random.randint(ks[1], (BATCH, 1), 0, 4096, dtype=jnp.int32))
    return {
        "x": x,
        "positions": positions,
        "w_in": n(2, (L, D_MODEL, IN_TOTAL)) * D_MODEL ** -0.5,
        "conv_w": n(3, (L, CONV_K, CONV_DIM)) * CONV_K ** -0.5,
        "q_norm_g": 1.0 + 0.02 * n(4, (L, Q_LORA)),
        "w_uq": n(5, (L, Q_LORA, MLA_HEADS * QK_HEAD)) * Q_LORA ** -0.5,
        "kv_norm_g": 1.0 + 0.02 * n(6, (L, KV_LORA)),
        "w_ukv": n(7, (L, KV_LORA, MLA_HEADS * (QK_NOPE + V_DIM))) * KV_LORA ** -0.5,
        "rwkv_mu": jax.random.uniform(ks[8], (L, RWKV_SHIFT_DIM), jnp.float32),
        "rwkv_w0": 0.3 * n(9, (L, RWKV_DIM)),
        "rwkv_w2": n(10, (L, DECAY_LORA, RWKV_DIM)) * 0.5 * DECAY_LORA ** -0.5,
        "rwkv_a0": 0.3 * n(11, (L, RWKV_DIM)),
        "rwkv_a2": n(12, (L, A_LORA, RWKV_DIM)) * 0.5 * A_LORA ** -0.5,
        "rwkv_k_k": 0.85 + 0.05 * n(13, (L, RWKV_DIM)),
        "rwkv_k_a": 1.0 + 0.05 * n(14, (L, RWKV_DIM)),
        "rwkv_r_k": 0.1 * n(15, (L, RWKV_HEADS, RWKV_N)),
        "rwkv_gn_g": 1.0 + 0.02 * n(16, (L, RWKV_DIM)),
        "rwkv_gn_b": 0.02 * n(17, (L, RWKV_DIM)),
        "w_out": n(18, (L, MIX_DIM, D_MODEL)) * MIX_DIM ** -0.5 * DEEPNORM_BETA,
        "ln_g": 1.0 + 0.02 * n(19, (L, D_MODEL)),
        "ln_b": 0.02 * n(20, (L, D_MODEL)),
    }


def reference(x, positions, w_in, conv_w, q_norm_g, w_uq, kv_norm_g, w_ukv,
              rwkv_mu, rwkv_w0, rwkv_w2, rwkv_a0, rwkv_a2, rwkv_k_k, rwkv_k_a,
              rwkv_r_k, rwkv_gn_g, rwkv_gn_b, w_out, ln_g, ln_b):
    inv_freq = ROPE_THETA ** (-jnp.arange(0, QK_ROPE, 2, dtype=jnp.float32) / QK_ROPE)
    ang = positions.astype(jnp.float32)[..., None] * inv_freq
    cos = jnp.cos(ang)[:, :, None, :].astype(x.dtype)
    sin = jnp.sin(ang)[:, :, None, :].astype(x.dtype)

    for l in range(DEPTH):
        u = x @ w_in[l]
        (cb, cc, ch, cg, cq, ckv, kpe, mg, rcols, rg) = _split(u, IN_SIZES)
        y_conv = _short_conv_branch(cb, cc, ch, cg, conv_w[l])
        y_mla = _mla_branch(cq, ckv, kpe, mg, q_norm_g[l], w_uq[l],
                            kv_norm_g[l], w_ukv[l], cos, sin)
        y_rwkv = _rwkv7_branch(rcols, rg, rwkv_mu[l], rwkv_w0[l], rwkv_w2[l],
                               rwkv_a0[l], rwkv_a2[l], rwkv_k_k[l], rwkv_k_a[l],
                               rwkv_r_k[l], rwkv_gn_g[l], rwkv_gn_b[l])
        mix = jnp.concatenate([y_conv, y_mla, y_rwkv], axis=-1)
        x = _layer_norm(DEEPNORM_ALPHA * x + mix @ w_out[l], ln_g[l], ln_b[l])
    return x
```

```python
import functools

import jax
import jax.numpy as jnp
from jax import lax
from jax.experimental import pallas as pl
from jax.experimental.pallas import tpu as pltpu

F32 = jnp.float32
BF16 = jnp.bfloat16

D_MODEL = 2048
CONV_DIM = 512
CONV_K = 3
MLA_HEADS = 8
QK_NOPE = 128
QK_ROPE = 64
QK_HEAD = QK_NOPE + QK_ROPE
V_DIM = 128
MLA_DIM = MLA_HEADS * V_DIM
Q_LORA = 512
KV_LORA = 256
ROPE_THETA = 10000.0
RWKV_HEADS = 8
RWKV_N = 64
RWKV_DIM = RWKV_HEADS * RWKV_N
DECAY_LORA = 64
A_LORA = 64
RWKV_GN_EPS = 64e-5
LN_EPS = 1e-5
RMS_EPS = 1e-6
DEPTH = 2
DEEPNORM_ALPHA = (2 * DEPTH) ** 0.25

U_CB, U_CC, U_CH, U_CG = 0, 512, 1024, 1536
U_CQ = 2048
U_R, U_K, U_V, U_RG = 2560, 3072, 3584, 4096
U_MG = 4608
U_CKV = 5632
U_KPE = 5888
U_WA = 6016
U_TOTAL = 6144

LANES = 128
CHUNK = 64
VMEM_LIMIT = 56 * 1024 * 1024

LOG2E = 1.4426950408889634


def _cparams(sem):
    return pltpu.CompilerParams(dimension_semantics=sem, vmem_limit_bytes=VMEM_LIMIT)


def _bdot(a, b):
    return jnp.dot(a.astype(BF16), b.astype(BF16), preferred_element_type=F32)


def _inproj_kernel(x_ref, w_ref, o_ref, xb_ref):
    @pl.when(pl.program_id(1) == 0)
    def _():
        xb_ref[...] = x_ref[...].astype(BF16)

    o_ref[...] = jnp.dot(xb_ref[...], w_ref[...], preferred_element_type=F32)


def _inproj(x2, w_p, tm, tn):
    t = x2.shape[0]
    return pl.pallas_call(
        _inproj_kernel,
        grid=(t // tm, U_TOTAL // tn),
        in_specs=[pl.BlockSpec((tm, D_MODEL), lambda i, j: (i, 0)),
                  pl.BlockSpec((D_MODEL, tn), lambda i, j: (0, j))],
        out_specs=pl.BlockSpec((tm, tn), lambda i, j: (i, j)),
        out_shape=jax.ShapeDtypeStruct((t, U_TOTAL), F32),
        scratch_shapes=[pltpu.VMEM((tm, D_MODEL), BF16)],
        compiler_params=_cparams(("parallel", "arbitrary")),
        name="inproj",
    )(x2, w_p)


def _shift_rows(x, prev, n, row):
    out = pltpu.roll(x, n, 0)
    for r in range(n):
        out = jnp.where(row == r, prev[8 - n + r:8 - n + r + 1], out)
    return out


def _conv_kernel(cb, cc, ch, cg, ccp, chp, w, o, *, tiles_per_seq):
    first = (pl.program_id(0) % tiles_per_seq) == 0
    up = cc[...] * ch[...]
    prev = jnp.where(first, 0.0, ccp[...] * chp[...])
    row = lax.broadcasted_iota(jnp.int32, up.shape, 0)
    u1 = _shift_rows(up, prev, 1, row)
    u2 = _shift_rows(up, prev, 2, row)
    wv = w[...]
    y = wv[0:1] * u2 + wv[1:2] * u1 + wv[2:3] * up
    g = cg[...]
    o[...] = (cb[...] * y * (g * jax.nn.sigmoid(g))).astype(o.dtype)


def _conv_branch(u, conv_w, seq, tm):
    t = u.shape[0]
    cblk = lambda c: pl.BlockSpec((tm, CONV_DIM), lambda i, c=c: (i, c))
    pblk = lambda c: pl.BlockSpec((8, CONV_DIM), lambda i, c=c: (jnp.maximum(i * (tm // 8) - 1, 0), c))
    return pl.pallas_call(
        functools.partial(_conv_kernel, tiles_per_seq=seq // tm),
        grid=(t // tm,),
        in_specs=[cblk(U_CB // 512), cblk(U_CC // 512), cblk(U_CH // 512), cblk(U_CG // 512),
                  pblk(U_CC // 512), pblk(U_CH // 512),
                  pl.BlockSpec((CONV_K, CONV_DIM), lambda i: (0, 0))],
        out_specs=pl.BlockSpec((tm, CONV_DIM), lambda i: (i, 0)),
        out_shape=jax.ShapeDtypeStruct((t, CONV_DIM), BF16),
        compiler_params=_cparams(("parallel",)),
        name="conv_branch",
    )(u, u, u, u, u, u, conv_w)


def _rope_kernel(pos_ref, invf_ref, cs_ref):
    ang = pos_ref[...].astype(F32) * invf_ref[...]
    lane = lax.broadcasted_iota(jnp.int32, ang.shape, 1)
    cs_ref[...] = jnp.where(lane < QK_ROPE, jnp.cos(ang), jnp.sin(ang))


def _rope_table(pos2, invf4, tm):
    t = pos2.shape[0]
    return pl.pallas_call(
        _rope_kernel,
        grid=(t // tm,),
        in_specs=[pl.BlockSpec((tm, 1), lambda i: (i, 0)),
                  pl.BlockSpec((1, LANES), lambda i: (0, 0))],
        out_specs=pl.BlockSpec((tm, LANES), lambda i: (i, 0)),
        out_shape=jax.ShapeDtypeStruct((t, LANES), F32),
        compiler_params=_cparams(("parallel",)),
        name="rope_table",
    )(pos2, invf4)


def _rms(x, g):
    return x * lax.rsqrt(jnp.mean(x * x, -1, keepdims=True) + RMS_EPS) * g


def _mla_prep_kernel(cq_ref, ckv_ref, kp_ref, cs_ref, qg_ref, wq_ref, kg_ref, wkv_ref,
                     q_out, k_out, v_out, *, qscale):
    cs = cs_ref[...]
    qf = _bdot(_rms(cq_ref[...], qg_ref[...]), wq_ref[...])
    kvf = _bdot(_rms(ckv_ref[...], kg_ref[...]), wkv_ref[...])
    kp = kp_ref[...] * cs
    kpe = (kp + pltpu.roll(kp, QK_ROPE, 1))[:, :QK_ROPE].astype(BF16)
    for h in range(MLA_HEADS):
        c0 = 2 * LANES * h
        pp = qf[:, c0 + LANES:c0 + 2 * LANES] * cs
        pe = pp + pltpu.roll(pp, QK_ROPE, 1)
        q_out[h, :, 0:QK_NOPE] = (qf[:, c0:c0 + LANES] * qscale).astype(BF16)
        q_out[h, :, QK_NOPE:QK_HEAD] = (pe[:, :QK_ROPE] * qscale).astype(BF16)
        k_out[h, :, 0:QK_NOPE] = kvf[:, c0:c0 + LANES].astype(BF16)
        k_out[h, :, QK_NOPE:QK_HEAD] = kpe
        v_out[h] = kvf[:, c0 + LANES:c0 + 2 * LANES].astype(BF16)


def _mla_prep(u, cs, qg, wq_p, kg, wkv, batch, seq, tm):
    t = u.shape[0]
    nt = seq // tm
    hblk = lambda d: pl.BlockSpec((None, MLA_HEADS, tm, d), lambda i: (i // nt, 0, i % nt, 0))
    full = lambda a: pl.BlockSpec(a.shape, lambda i: (0,) * a.ndim)
    qscale = QK_HEAD ** -0.5 * LOG2E
    return pl.pallas_call(
        functools.partial(_mla_prep_kernel, qscale=qscale),
        grid=(t // tm,),
        in_specs=[pl.BlockSpec((tm, Q_LORA), lambda i: (i, U_CQ // Q_LORA)),
                  pl.BlockSpec((tm, KV_LORA), lambda i: (i, U_CKV // KV_LORA)),
                  pl.BlockSpec((tm, LANES), lambda i: (i, U_KPE // LANES)),
                  pl.BlockSpec((tm, LANES), lambda i: (i, 0)),
                  full(qg), full(wq_p), full(kg), full(wkv)],
        out_specs=[hblk(QK_HEAD), hblk(QK_HEAD), hblk(V_DIM)],
        out_shape=[jax.ShapeDtypeStruct((batch, MLA_HEADS, seq, QK_HEAD), BF16),
                   jax.ShapeDtypeStruct((batch, MLA_HEADS, seq, QK_HEAD), BF16),
                   jax.ShapeDtypeStruct((batch, MLA_HEADS, seq, V_DIM), BF16)],
        compiler_params=_cparams(("parallel",)),
        name="mla_prep",
    )(u, u, u, cs, qg, wq_p, kg, wkv)


def _attn_kernel(q_ref, k_ref, v_ref, g_ref, o_ref, m_ref, l_ref, acc_ref, *, tq):
    i = pl.program_id(2)
    q = q_ref[...]
    m_ref[...] = jnp.full(m_ref.shape, -jnp.inf, F32)
    l_ref[...] = jnp.zeros(l_ref.shape, F32)
    acc_ref[...] = jnp.zeros(acc_ref.shape, F32)

    def block(j, masked):
        start = pl.multiple_of(j * tq, tq)
        kb = k_ref[pl.ds(start, tq), :]
        vb = v_ref[pl.ds(start, tq), :]
        s = lax.dot_general(q, kb, (((1,), (1,)), ((), ())), preferred_element_type=F32)
        if masked:
            r = lax.broadcasted_iota(jnp.int32, s.shape, 0)
            c = lax.broadcasted_iota(jnp.int32, s.shape, 1)
            s = jnp.where(c <= r, s, -jnp.inf)
        m_prev = m_ref[...]
        m_new = jnp.maximum(m_prev, jnp.max(s, axis=1, keepdims=True))
        p = jnp.exp2(s - m_new)
        alpha = jnp.exp2(m_prev - m_new)
        l_ref[...] = alpha * l_ref[...] + jnp.sum(p, axis=1, keepdims=True)
        acc_ref[...] = alpha * acc_ref[...] + jnp.dot(p.astype(BF16), vb, preferred_element_type=F32)
        m_ref[...] = m_new

    def body(j, carry):
        block(j, False)
        return carry

    lax.fori_loop(0, i, body, 0)
    block(i, True)
    g = g_ref[...]
    o_ref[...] = (acc_ref[...] / l_ref[...] * (g * jax.nn.sigmoid(g))).astype(o_ref.dtype)


def _attention(q, k, v, u, seq, tq):
    batch = q.shape[0]
    t = batch * seq
    nq = seq // tq
    return pl.pallas_call(
        functools.partial(_attn_kernel, tq=tq),
        grid=(batch, MLA_HEADS, nq),
        in_specs=[pl.BlockSpec((None, None, tq, QK_HEAD), lambda b, h, i: (b, h, i, 0)),
                  pl.BlockSpec((None, None, seq, QK_HEAD), lambda b, h, i: (b, h, 0, 0)),
                  pl.BlockSpec((None, None, seq, V_DIM), lambda b, h, i: (b, h, 0, 0)),
                  pl.BlockSpec((tq, V_DIM), lambda b, h, i: (b * nq + i, U_MG // V_DIM + h))],
        out_specs=pl.BlockSpec((tq, V_DIM), lambda b, h, i: (b * nq + i, h)),
        out_shape=jax.ShapeDtypeStruct((t, MLA_DIM), BF16),
        scratch_shapes=[pltpu.VMEM((tq, 1), F32), pltpu.VMEM((tq, 1), F32),
                        pltpu.VMEM((tq, V_DIM), F32)],
        compiler_params=_cparams(("parallel", "parallel", "arbitrary")),
        name="mla_attention",
    )(q, k, v, u)


def _softplus(z):
    return jnp.maximum(z, 0.0) + jnp.log1p(jnp.exp(-jnp.abs(z)))


def _rwkv_prep_kernel(r_ref, k_ref, v_ref, wa_ref, rp_ref, kp_ref, vp_ref, wap_ref,
                      mu_r, mu_k, mu_v, mu_wa, w0, w2p, a0, a2p, kkw, kaw, eones,
                      r_o, lw_o, k_o, v_o, a_o, b_o, *, tiles_per_seq):
    first = (pl.program_id(0) % tiles_per_seq) == 0
    row = lax.broadcasted_iota(jnp.int32, r_ref.shape, 0)
    row_wa = lax.broadcasted_iota(jnp.int32, wa_ref.shape, 0)

    def lerp(x_ref, p_ref, mu, rw):
        x = x_ref[...]
        prev = jnp.where(first, 0.0, p_ref[...])
        return x + (_shift_rows(x, prev, 1, rw) - x) * mu[...]

    r = lerp(r_ref, rp_ref, mu_r, row)
    k = lerp(k_ref, kp_ref, mu_k, row)
    v = lerp(v_ref, vp_ref, mu_v, row)
    wa = lerp(wa_ref, wap_ref, mu_wa, row_wa)
    w = -_softplus(-(w0[...] + _bdot(jnp.tanh(wa), w2p[...]))) - 0.5
    a_sig = jax.nn.sigmoid(a0[...] + _bdot(wa, a2p[...]))
    kk = k * kkw[...]
    ss = _bdot(kk * kk, eones[...])
    kk = kk * (1.0 / jnp.maximum(jnp.sqrt(ss), 1e-12))
    r_o[...] = r
    lw_o[...] = -jnp.exp(w)
    k_o[...] = k * (1.0 + (a_sig - 1.0) * kaw[...])
    v_o[...] = v
    a_o[...] = -kk
    b_o[...] = kk * a_sig


def _rwkv_prep(u, mu_r, mu_k, mu_v, mu_wa, w0, w2p, a0, a2p, kkw, kaw, eones, seq, tm):
    t = u.shape[0]
    cblk = lambda c: pl.BlockSpec((tm, RWKV_DIM), lambda i, c=c: (i, c))
    pblk = lambda c: pl.BlockSpec((8, RWKV_DIM), lambda i, c=c: (jnp.maximum(i * (tm // 8) - 1, 0), c))
    full = lambda a: pl.BlockSpec(a.shape, lambda i: (0,) * a.ndim)
    out = jax.ShapeDtypeStruct((t, RWKV_DIM), F32)
    oblk = pl.BlockSpec((tm, RWKV_DIM), lambda i: (i, 0))
    return pl.pallas_call(
        functools.partial(_rwkv_prep_kernel, tiles_per_seq=seq // tm),
        grid=(t // tm,),
        in_specs=[cblk(U_R // 512), cblk(U_K // 512), cblk(U_V // 512),
                  pl.BlockSpec((tm, LANES), lambda i: (i, U_WA // LANES)),
                  pblk(U_R // 512), pblk(U_K // 512), pblk(U_V // 512),
                  pl.BlockSpec((8, LANES), lambda i: (jnp.maximum(i * (tm // 8) - 1, 0), U_WA // LANES)),
                  full(mu_r), full(mu_k), full(mu_v), full(mu_wa), full(w0), full(w2p),
                  full(a0), full(a2p), full(kkw), full(kaw), full(eones)],
        out_specs=[oblk] * 6,
        out_shape=[out] * 6,
        compiler_params=_cparams(("parallel",)),
        name="rwkv_prep",
    )(u, u, u, u, u, u, u, u, mu_r, mu_k, mu_v, mu_wa, w0, w2p, a0, a2p, kkw, kaw, eones)


def _stack(x, even_lane):
    return jnp.concatenate([jnp.where(even_lane, x, 0.0), jnp.where(even_lane, 0.0, x)], axis=0)


def _unstack(x):
    return x[:CHUNK] + x[CHUNK:]


def _chunk_terms(r, lw, k, v, a, b, masks):
    even_lane, strict, incl, same_blk, eye, tri = masks
    lw_hi = lw.astype(BF16)
    lw_lo = (lw - lw_hi.astype(F32)).astype(BF16)
    cum = (jnp.dot(tri, lw_hi, preferred_element_type=F32)
           + jnp.dot(tri, lw_lo, preferred_element_type=F32))
    cum_end = cum[CHUNK - 1:CHUNK]
    p_inv = jnp.exp(-cum)
    to_end = jnp.exp(cum_end - cum)
    at = a * jnp.exp(cum - lw)
    rt = r * jnp.exp(cum)
    bt = b * p_inv
    kt = k * p_inv
    at_s, rt_s, v_s = _stack(at, even_lane), _stack(rt, even_lane), _stack(v, even_lane)

    lhs = jnp.concatenate([at_s, rt_s], axis=0).astype(BF16)
    rhs = jnp.concatenate([bt, bt, kt, kt], axis=0).astype(BF16)
    g = lax.dot_general(lhs, rhs, (((1,), (1,)), ((), ())), preferred_element_type=F32)
    a_ab = jnp.where(strict, g[:LANES, :LANES], 0.0)
    a_ak = jnp.where(strict, g[:LANES, LANES:], 0.0)
    a_rb = jnp.where(incl, g[LANES:, :LANES], 0.0)
    a_rk = jnp.where(incl, g[LANES:, LANES:], 0.0)

    x = a_ab
    tinv = jnp.where(eye, 1.0, 0.0) + x
    n = 2
    while n < CHUNK:
        x = _bdot(x, x)
        tinv = tinv + _bdot(tinv, x)
        n *= 2

    akv = _bdot(a_ak, v_s)
    wu = _bdot(tinv, jnp.concatenate([at_s, akv], axis=1))
    w_s, u0_s = wu[:, :LANES], wu[:, LANES:]
    zeros = jnp.zeros((LANES, LANES), F32)
    big = jnp.concatenate([wu, jnp.concatenate([zeros, v_s], axis=1)], axis=0)
    top = _bdot(jnp.concatenate([a_rb, a_rk], axis=1), big)
    rw_s = rt_s + top[:, :LANES]
    y0_s = top[:, LANES:]

    bk = jnp.concatenate([b * to_end, k * to_end], axis=0)
    rhs2 = jnp.concatenate(
        [jnp.concatenate([_unstack(w_s), _unstack(u0_s)], axis=1),
         jnp.concatenate([jnp.zeros((CHUNK, LANES), F32), v], axis=1)], axis=0)
    mz = lax.dot_general(bk.astype(BF16), rhs2.astype(BF16), (((0,), (0,)), ((), ())),
                         preferred_element_type=F32)
    m = jnp.where(same_blk, mz[:, :LANES], 0.0) + jnp.where(eye, jnp.exp(cum_end), 0.0)
    z0 = jnp.where(same_blk, mz[:, LANES:], 0.0)
    return rw_s, y0_s, m, z0


def _rwkv_scan_kernel(r_ref, lw_ref, k_ref, v_ref, a_ref, b_ref, g_ref, gng, gnb, rkw, eones,
                      o_ref, state_ref, y_ref, *, tt):
    @pl.when(pl.program_id(1) == 0)
    def _():
        state_ref[...] = jnp.zeros(state_ref.shape, F32)

    rr = lax.broadcasted_iota(jnp.int32, (LANES, LANES), 0)
    cc = lax.broadcasted_iota(jnp.int32, (LANES, LANES), 1)
    same_blk = (rr // CHUNK) == (cc // CHUNK)
    strict = same_blk & ((cc % CHUNK) < (rr % CHUNK))
    incl = same_blk & ((cc % CHUNK) <= (rr % CHUNK))
    eye = rr == cc
    even_lane = lax.broadcasted_iota(jnp.int32, (CHUNK, LANES), 1) < CHUNK
    tr = lax.broadcasted_iota(jnp.int32, (CHUNK, CHUNK), 0)
    tc = lax.broadcasted_iota(jnp.int32, (CHUNK, CHUNK), 1)
    tri = jnp.where(tc <= tr, 1.0, 0.0).astype(BF16)
    masks = (even_lane, strict, incl, same_blk, eye, tri)

    def chunk_body(c, carry):
        rows = pl.ds(pl.multiple_of(c * CHUNK, CHUNK), CHUNK)
        for p in range(RWKV_DIM // LANES):
            cols = slice(p * LANES, (p + 1) * LANES)
            rw_s, y0_s, m, z0 = _chunk_terms(
                r_ref[rows, cols], lw_ref[rows, cols], k_ref[rows, cols],
                v_ref[rows, cols], a_ref[rows, cols], b_ref[rows, cols], masks)
            st = state_ref[p]
            ys = _bdot(jnp.concatenate([rw_s, m], axis=0), st)
            y_ref[rows, cols] = _unstack(ys[:LANES] + y0_s)
            state_ref[p] = ys[LANES:] + z0
        return carry

    lax.fori_loop(0, tt // CHUNK, chunk_body, 0)

    e = eones[...]

    def headsum(x):
        hi = x.astype(BF16)
        lo = (x - hi.astype(F32)).astype(BF16)
        return (jnp.dot(hi, e, preferred_element_type=F32)
                + jnp.dot(lo, e, preferred_element_type=F32))

    y = y_ref[...]
    d = y - headsum(y) * (1.0 / RWKV_N)
    var = headsum(d * d) * (1.0 / RWKV_N)
    yn = d * lax.rsqrt(var + RWKV_GN_EPS) * gng[...] + gnb[...]
    yn = yn + headsum(r_ref[...] * k_ref[...] * rkw[...]) * v_ref[...]
    g = g_ref[...]
    o_ref[...] = (yn * (g * jax.nn.sigmoid(g))).astype(o_ref.dtype)


def _rwkv_scan(r, lw, k, v, a, b, u, gng, gnb, rkw, eones, batch, seq, tt):
    t = batch * seq
    nt = seq // tt
    blk = pl.BlockSpec((tt, RWKV_DIM), lambda bi, i: (bi * nt + i, 0))
    full = lambda x: pl.BlockSpec(x.shape, lambda bi, i: (0,) * x.ndim)
    return pl.pallas_call(
        functools.partial(_rwkv_scan_kernel, tt=tt),
        grid=(batch, nt),
        in_specs=[blk] * 6 + [pl.BlockSpec((tt, RWKV_DIM), lambda bi, i: (bi * nt + i, U_RG // RWKV_DIM)),
                              full(gng), full(gnb), full(rkw), full(eones)],
        out_specs=blk,
        out_shape=jax.ShapeDtypeStruct((t, RWKV_DIM), BF16),
        scratch_shapes=[pltpu.VMEM((RWKV_DIM // LANES, LANES, LANES), F32),
                        pltpu.VMEM((tt, RWKV_DIM), F32)],
        compiler_params=_cparams(("parallel", "arbitrary")),
        name="rwkv_scan",
    )(r, lw, k, v, a, b, u, gng, gnb, rkw, eones)


def _outproj_kernel(yc, ym, yr, x_ref, w1, w2, w3, lg, lb, o_ref):
    acc = jnp.dot(yc[...], w1[...], preferred_element_type=F32)
    acc = acc + jnp.dot(ym[...], w2[...], preferred_element_type=F32)
    acc = acc + jnp.dot(yr[...], w3[...], preferred_element_type=F32)
    z = DEEPNORM_ALPHA * x_ref[...] + acc
    mu = jnp.mean(z, -1, keepdims=True)
    d = z - mu
    var = jnp.mean(d * d, -1, keepdims=True)
    o_ref[...] = d * lax.rsqrt(var + LN_EPS) * lg[...] + lb[...]


def _outproj(yc, ym, yr, x2, w1, w2, w3, lg, lb, tm):
    t = x2.shape[0]
    rblk = lambda d: pl.BlockSpec((tm, d), lambda i: (i, 0))
    full = lambda a: pl.BlockSpec(a.shape, lambda i: (0,) * a.ndim)
    return pl.pallas_call(
        _outproj_kernel,
        grid=(t // tm,),
        in_specs=[rblk(CONV_DIM), rblk(MLA_DIM), rblk(RWKV_DIM), rblk(D_MODEL),
                  full(w1), full(w2), full(w3), full(lg), full(lb)],
        out_specs=rblk(D_MODEL),
        out_shape=jax.ShapeDtypeStruct((t, D_MODEL), F32),
        compiler_params=_cparams(("parallel",)),
        name="outproj_ln",
    )(yc, ym, yr, x2, w1, w2, w3, lg, lb)


def _rot_half_cols(w):
    return jnp.concatenate([-w[:, QK_ROPE // 2:], w[:, :QK_ROPE // 2]], axis=1)


def _pack_w_in(w):
    o_ckv, o_kpe, o_mg, o_rc = 2560, 2816, 2880, 3904
    ckv = w[:, o_ckv:o_kpe]
    kpe = w[:, o_kpe:o_mg]
    mg = w[:, o_mg:o_rc]
    r = w[:, o_rc:o_rc + 512]
    wd = w[:, o_rc + 512:o_rc + 576]
    k = w[:, o_rc + 576:o_rc + 1088]
    v = w[:, o_rc + 1088:o_rc + 1600]
    ad = w[:, o_rc + 1600:o_rc + 1664]
    rg = w[:, o_rc + 1664:]
    return jnp.concatenate([w[:, :o_ckv], r, k, v, rg, mg, ckv, kpe, _rot_half_cols(kpe),
                            wd, ad], axis=1).astype(BF16)


def _pack_mu(mu):
    r, wd, k, v, ad = (mu[0:512], mu[512:576], mu[576:1088], mu[1088:1600], mu[1600:1664])
    row = lambda a: a.reshape(1, -1)
    return row(r), row(k), row(v), row(jnp.concatenate([wd, ad]))


def _pack_w_uq(w):
    cols = []
    for h in range(MLA_HEADS):
        wh = w[:, h * QK_HEAD:(h + 1) * QK_HEAD]
        pe = wh[:, QK_NOPE:]
        cols += [wh[:, :QK_NOPE], pe, _rot_half_cols(pe)]
    return jnp.concatenate(cols, axis=1).astype(BF16)


def _tile(n, pref):
    return pref if n % pref == 0 else n


def kernel(x, positions, w_in, conv_w, q_norm_g, w_uq, kv_norm_g, w_ukv, rwkv_mu, rwkv_w0,
           rwkv_w2, rwkv_a0, rwkv_a2, rwkv_k_k, rwkv_k_a, rwkv_r_k, rwkv_gn_g, rwkv_gn_b,
           w_out, ln_g, ln_b):
    batch, seq, _ = x.shape
    t = batch * seq
    row = lambda a: a.reshape(1, -1)

    inv_freq = ROPE_THETA ** (-jnp.arange(0, QK_ROPE, 2, dtype=F32) / QK_ROPE)
    invf4 = jnp.tile(inv_freq, 4).reshape(1, LANES)
    cs = _rope_table(positions.reshape(t, 1), invf4, _tile(t, 1024))

    head_of = jnp.arange(RWKV_DIM) // RWKV_N
    eones = (head_of[:, None] == head_of[None, :]).astype(BF16)
    zpad = jnp.zeros((DECAY_LORA, RWKV_DIM), F32)

    x2 = x.reshape(t, D_MODEL)
    for l in range(DEPTH):
        u = _inproj(x2, _pack_w_in(w_in[l]), _tile(t, 512), 768)
        y_conv = _conv_branch(u, conv_w[l], seq, _tile(seq, 512))
        q, k, v = _mla_prep(u, cs, row(q_norm_g[l]), _pack_w_uq(w_uq[l]), row(kv_norm_g[l]),
                            w_ukv[l].astype(BF16), batch, seq, _tile(seq, 512))
        y_mla = _attention(q, k, v, u, seq, _tile(seq, 512))
        mu_r, mu_k, mu_v, mu_wa = _pack_mu(rwkv_mu[l])
        w2p = jnp.concatenate([rwkv_w2[l], zpad], axis=0).astype(BF16)
        a2p = jnp.concatenate([zpad, rwkv_a2[l]], axis=0).astype(BF16)
        rr, lw, kk, vv, aa, bb = _rwkv_prep(
            u, mu_r, mu_k, mu_v, mu_wa, row(rwkv_w0[l]), w2p, row(rwkv_a0[l]), a2p,
            row(rwkv_k_k[l]), row(rwkv_k_a[l]), eones, seq, _tile(seq, 512))
        y_rwkv = _rwkv_scan(rr, lw, kk, vv, aa, bb, u, row(rwkv_gn_g[l]), row(rwkv_gn_b[l]),
                            row(rwkv_r_k[l]), eones, batch, seq, _tile(seq, 512))
        wo = w_out[l].astype(BF16)
        x2 = _outproj(y_conv, y_mla, y_rwkv, x2,
                      wo[:CONV_DIM], wo[CONV_DIM:CONV_DIM + MLA_DIM], wo[CONV_DIM + MLA_DIM:],
                      row(ln_g[l]), row(ln_b[l]), _tile(t, 256))
    return x2.reshape(batch, seq, D_MODEL)
```

```python
import functools

import jax
import jax.numpy as jnp
from jax import lax
from jax.experimental import pallas as pl
from jax.experimental.pallas import tpu as pltpu

F32 = jnp.float32
BF16 = jnp.bfloat16

D_MODEL = 2048
CONV_DIM = 512
CONV_K = 3
MLA_HEADS = 8
QK_NOPE = 128
QK_ROPE = 64
QK_HEAD = QK_NOPE + QK_ROPE
V_DIM = 128
MLA_DIM = MLA_HEADS * V_DIM
Q_LORA = 512
KV_LORA = 256
ROPE_THETA = 10000.0
RWKV_HEADS = 8
RWKV_N = 64
RWKV_DIM = RWKV_HEADS * RWKV_N
DECAY_LORA = 64
A_LORA = 64
RWKV_GN_EPS = 64e-5
LN_EPS = 1e-5
RMS_EPS = 1e-6
DEPTH = 2
DEEPNORM_ALPHA = (2 * DEPTH) ** 0.25

U_CB, U_CC, U_CH, U_CG = 0, 512, 1024, 1536
U_CQ = 2048
U_R, U_K, U_V, U_RG = 2560, 3072, 3584, 4096
U_MG = 4608
U_CKV = 5632
U_KPE = 5888
U_WA = 6016
U_TOTAL = 6144

LANES = 128
CHUNK = 64
VMEM_LIMIT = 56 * 1024 * 1024

LOG2E = 1.4426950408889634


def _cparams(sem):
    return pltpu.CompilerParams(dimension_semantics=sem, vmem_limit_bytes=VMEM_LIMIT)


def _bdot(a, b):
    return jnp.dot(a.astype(BF16), b.astype(BF16), preferred_element_type=F32)


def _inproj_kernel(x_ref, w_ref, o_ref, xb_ref):
    @pl.when(pl.program_id(1) == 0)
    def _():
        xb_ref[...] = x_ref[...].astype(BF16)

    o_ref[...] = jnp.dot(xb_ref[...], w_ref[...], preferred_element_type=F32)


def _inproj(x2, w_p, tm, tn):
    t = x2.shape[0]
    return pl.pallas_call(
        _inproj_kernel,
        grid=(t // tm, U_TOTAL // tn),
        in_specs=[pl.BlockSpec((tm, D_MODEL), lambda i, j: (i, 0)),
                  pl.BlockSpec((D_MODEL, tn), lambda i, j: (0, j))],
        out_specs=pl.BlockSpec((tm, tn), lambda i, j: (i, j)),
        out_shape=jax.ShapeDtypeStruct((t, U_TOTAL), F32),
        scratch_shapes=[pltpu.VMEM((tm, D_MODEL), BF16)],
        compiler_params=_cparams(("parallel", "arbitrary")),
        name="inproj",
    )(x2, w_p)


def _shift_rows(x, prev, n, row):
    out = pltpu.roll(x, n, 0)
    for r in range(n):
        out = jnp.where(row == r, prev[8 - n + r:8 - n + r + 1], out)
    return out


def _conv_kernel(cb, cc, ch, cg, ccp, chp, w, o, *, tiles_per_seq):
    first = (pl.program_id(0) % tiles_per_seq) == 0
    up = cc[...] * ch[...]
    prev = jnp.where(first, 0.0, ccp[...] * chp[...])
    row = lax.broadcasted_iota(jnp.int32, up.shape, 0)
    u1 = _shift_rows(up, prev, 1, row)
    u2 = _shift_rows(up, prev, 2, row)
    wv = w[...]
    y = wv[0:1] * u2 + wv[1:2] * u1 + wv[2:3] * up
    g = cg[...]
    o[...] = (cb[...] * y * (g * jax.nn.sigmoid(g))).astype(o.dtype)


def _conv_branch(u, conv_w, seq, tm):
    t = u.shape[0]
    cblk = lambda c: pl.BlockSpec((tm, CONV_DIM), lambda i, c=c: (i, c))
    pblk = lambda c: pl.BlockSpec((8, CONV_DIM), lambda i, c=c: (jnp.maximum(i * (tm // 8) - 1, 0), c))
    return pl.pallas_call(
        functools.partial(_conv_kernel, tiles_per_seq=seq // tm),
        grid=(t // tm,),
        in_specs=[cblk(U_CB // 512), cblk(U_CC // 512), cblk(U_CH // 512), cblk(U_CG // 512),
                  pblk(U_CC // 512), pblk(U_CH // 512),
                  pl.BlockSpec((CONV_K, CONV_DIM), lambda i: (0, 0))],
        out_specs=pl.BlockSpec((tm, CONV_DIM), lambda i: (i, 0)),
        out_shape=jax.ShapeDtypeStruct((t, CONV_DIM), BF16),
        compiler_params=_cparams(("parallel",)),
        name="conv_branch",
    )(u, u, u, u, u, u, conv_w)


def _rope_kernel(pos_ref, invf_ref, cs_ref):
    ang = pos_ref[...].astype(F32) * invf_ref[...]
    lane = lax.broadcasted_iota(jnp.int32, ang.shape, 1)
    cs_ref[...] = jnp.where(lane < QK_ROPE, jnp.cos(ang), jnp.sin(ang))


def _rope_table(pos2, invf4, tm):
    t = pos2.shape[0]
    return pl.pallas_call(
        _rope_kernel,
        grid=(t // tm,),
        in_specs=[pl.BlockSpec((tm, 1), lambda i: (i, 0)),
                  pl.BlockSpec((1, LANES), lambda i: (0, 0))],
        out_specs=pl.BlockSpec((tm, LANES), lambda i: (i, 0)),
        out_shape=jax.ShapeDtypeStruct((t, LANES), F32),
        compiler_params=_cparams(("parallel",)),
        name="rope_table",
    )(pos2, invf4)


def _rms(x, g):
    return x * lax.rsqrt(jnp.mean(x * x, -1, keepdims=True) + RMS_EPS) * g


def _mla_prep_kernel(cq_ref, ckv_ref, kp_ref, cs_ref, qg_ref, wq_ref, kg_ref, wkv_ref,
                     q_out, k_out, v_out, *, qscale):
    cs = cs_ref[...]
    qf = _bdot(_rms(cq_ref[...], qg_ref[...]), wq_ref[...])
    kvf = _bdot(_rms(ckv_ref[...], kg_ref[...]), wkv_ref[...])
    kp = kp_ref[...] * cs
    kpe = (kp + pltpu.roll(kp, QK_ROPE, 1))[:, :QK_ROPE].astype(BF16)
    for h in range(MLA_HEADS):
        c0 = 2 * LANES * h
        pp = qf[:, c0 + LANES:c0 + 2 * LANES] * cs
        pe = pp + pltpu.roll(pp, QK_ROPE, 1)
        q_out[h, :, 0:QK_NOPE] = (qf[:, c0:c0 + LANES] * qscale).astype(BF16)
        q_out[h, :, QK_NOPE:QK_HEAD] = (pe[:, :QK_ROPE] * qscale).astype(BF16)
        k_out[h, :, 0:QK_NOPE] = kvf[:, c0:c0 + LANES].astype(BF16)
        k_out[h, :, QK_NOPE:QK_HEAD] = kpe
        v_out[h, :, 0:V_DIM] = kvf[:, c0 + LANES:c0 + 2 * LANES].astype(BF16)
        v_out[h, :, V_DIM:2 * V_DIM] = jnp.ones((kvf.shape[0], V_DIM), BF16)


def _mla_prep(u, cs, qg, wq_p, kg, wkv, batch, seq, tm):
    t = u.shape[0]
    nt = seq // tm
    hblk = lambda d: pl.BlockSpec((None, MLA_HEADS, tm, d), lambda i: (i // nt, 0, i % nt, 0))
    full = lambda a: pl.BlockSpec(a.shape, lambda i: (0,) * a.ndim)
    qscale = QK_HEAD ** -0.5 * LOG2E
    return pl.pallas_call(
        functools.partial(_mla_prep_kernel, qscale=qscale),
        grid=(t // tm,),
        in_specs=[pl.BlockSpec((tm, Q_LORA), lambda i: (i, U_CQ // Q_LORA)),
                  pl.BlockSpec((tm, KV_LORA), lambda i: (i, U_CKV // KV_LORA)),
                  pl.BlockSpec((tm, LANES), lambda i: (i, U_KPE // LANES)),
                  pl.BlockSpec((tm, LANES), lambda i: (i, 0)),
                  full(qg), full(wq_p), full(kg), full(wkv)],
        out_specs=[hblk(QK_HEAD), hblk(QK_HEAD), hblk(2 * V_DIM)],
        out_shape=[jax.ShapeDtypeStruct((batch, MLA_HEADS, seq, QK_HEAD), BF16),
                   jax.ShapeDtypeStruct((batch, MLA_HEADS, seq, QK_HEAD), BF16),
                   jax.ShapeDtypeStruct((batch, MLA_HEADS, seq, 2 * V_DIM), BF16)],
        compiler_params=_cparams(("parallel",)),
        name="mla_prep",
    )(u, u, u, cs, qg, wq_p, kg, wkv)


def _attn_kernel(q_ref, k_ref, v_ref, g_ref, o_ref, m_ref, acc_ref, s0, s1, p0, p1, a0, a1, *, tq, sub):
    tk = tq // 2
    i = pl.program_id(2)
    s_scr, p_scr, a_scr = (s0, s1), (p0, p1), (a0, a1)
    lo, hi, full = slice(0, tk), slice(tk, tq), slice(0, tq)
    nt = (((1,), (1,)), ((), ()))

    m_ref[...] = jnp.full(m_ref.shape, -jnp.inf, F32)
    acc_ref[...] = jnp.zeros(acc_ref.shape, F32)
    p1[...] = jnp.zeros(p1.shape, BF16)
    a1[...] = jnp.ones(a1.shape, F32)

    def qk(t, slot, rows):
        kb = k_ref[pl.ds(pl.multiple_of(t * tk, tk), tk), :]
        s_scr[slot][rows, :] = lax.dot_general(q_ref[rows, :], kb, nt, preferred_element_type=F32)

    def softmax(slot, rows, causal):
        for r0 in range(rows.start, rows.stop, sub):
            rs = slice(r0, r0 + sub)
            s = s_scr[slot][rs, :]
            if causal:
                r = lax.broadcasted_iota(jnp.int32, s.shape, 0) + (r0 - rows.start)
                c = lax.broadcasted_iota(jnp.int32, s.shape, 1)
                s = jnp.where(c <= r, s, -jnp.inf)
            m_prev = m_ref[rs, :]
            m_new = jnp.maximum(m_prev, jnp.max(s, axis=1, keepdims=True))
            p_scr[slot][rs, :] = jnp.exp2(s - pltpu.repeat(m_new, tk // LANES, 1)).astype(BF16)
            a_scr[slot][rs, :] = jnp.exp2(m_prev - m_new)
            m_ref[rs, :] = m_new

    def pv(t, slot, rows):
        vb = v_ref[pl.ds(pl.multiple_of(t * tk, tk), tk), :]
        acc_ref[rows, :] = (pltpu.repeat(a_scr[slot][rows, :], 2, 1) * acc_ref[rows, :]
                            + jnp.dot(p_scr[slot][rows, :], vb, preferred_element_type=F32))

    def step(t, slot):
        qk(t + 1, 1 - slot, full)
        pv(jnp.maximum(t - 1, 0), 1 - slot, full)
        softmax(slot, full, False)

    def pair(tt, carry):
        step(2 * tt, 0)
        step(2 * tt + 1, 1)
        return carry

    qk(0, 0, full)
    lax.fori_loop(0, i, pair, 0)
    d = 2 * i
    qk(d + 1, 1, hi)
    pv(jnp.maximum(d - 1, 0), 1, full)
    softmax(0, lo, True)
    softmax(0, hi, False)
    softmax(1, hi, True)
    pv(d, 0, full)
    pv(d + 1, 1, hi)
    g = g_ref[...]
    acc = acc_ref[...]
    o_ref[...] = (acc[:, :V_DIM] / acc[:, V_DIM:] * (g * jax.nn.sigmoid(g))).astype(o_ref.dtype)


def _attention(q, k, v_aug, u, seq, tq, sub):
    batch = q.shape[0]
    t = batch * seq
    nq = seq // tq
    return pl.pallas_call(
        functools.partial(_attn_kernel, tq=tq, sub=sub),
        grid=(batch, MLA_HEADS, nq),
        in_specs=[pl.BlockSpec((None, None, tq, QK_HEAD), lambda b, h, i: (b, h, i, 0)),
                  pl.BlockSpec((None, None, seq, QK_HEAD), lambda b, h, i: (b, h, 0, 0)),
                  pl.BlockSpec((None, None, seq, 2 * V_DIM), lambda b, h, i: (b, h, 0, 0)),
                  pl.BlockSpec((tq, V_DIM), lambda b, h, i: (b * nq + i, U_MG // V_DIM + h))],
        out_specs=pl.BlockSpec((tq, V_DIM), lambda b, h, i: (b * nq + i, h)),
        out_shape=jax.ShapeDtypeStruct((t, MLA_DIM), BF16),
        scratch_shapes=[pltpu.VMEM((tq, LANES), F32), pltpu.VMEM((tq, 2 * V_DIM), F32),
                        pltpu.VMEM((tq, tq // 2), F32), pltpu.VMEM((tq, tq // 2), F32),
                        pltpu.VMEM((tq, tq // 2), BF16), pltpu.VMEM((tq, tq // 2), BF16),
                        pltpu.VMEM((tq, LANES), F32), pltpu.VMEM((tq, LANES), F32)],
        compiler_params=_cparams(("parallel", "parallel", "arbitrary")),
        name="mla_attention",
    )(q, k, v_aug, u)


def _softplus(z):
    return jnp.maximum(z, 0.0) + jnp.log1p(jnp.exp(-jnp.abs(z)))


def _rwkv_prep_kernel(r_ref, k_ref, v_ref, wa_ref, rp_ref, kp_ref, vp_ref, wap_ref,
                      mu_r, mu_k, mu_v, mu_wa, w0, w2p, a0, a2p, kkw, kaw, eones,
                      r_o, lw_o, k_o, v_o, a_o, b_o, *, tiles_per_seq):
    first = (pl.program_id(0) % tiles_per_seq) == 0
    row = lax.broadcasted_iota(jnp.int32, r_ref.shape, 0)
    row_wa = lax.broadcasted_iota(jnp.int32, wa_ref.shape, 0)

    def lerp(x_ref, p_ref, mu, rw):
        x = x_ref[...]
        prev = jnp.where(first, 0.0, p_ref[...])
        return x + (_shift_rows(x, prev, 1, rw) - x) * mu[...]

    r = lerp(r_ref, rp_ref, mu_r, row)
    k = lerp(k_ref, kp_ref, mu_k, row)
    v = lerp(v_ref, vp_ref, mu_v, row)
    wa = lerp(wa_ref, wap_ref, mu_wa, row_wa)
    w = -_softplus(-(w0[...] + _bdot(jnp.tanh(wa), w2p[...]))) - 0.5
    a_sig = jax.nn.sigmoid(a0[...] + _bdot(wa, a2p[...]))
    kk = k * kkw[...]
    ss = _bdot(kk * kk, eones[...])
    kk = kk * (1.0 / jnp.maximum(jnp.sqrt(ss), 1e-12))
    r_o[...] = r
    lw_o[...] = -jnp.exp(w)
    k_o[...] = k * (1.0 + (a_sig - 1.0) * kaw[...])
    v_o[...] = v
    a_o[...] = -kk
    b_o[...] = kk * a_sig


def _rwkv_prep(u, mu_r, mu_k, mu_v, mu_wa, w0, w2p, a0, a2p, kkw, kaw, eones, seq, tm):
    t = u.shape[0]
    cblk = lambda c: pl.BlockSpec((tm, RWKV_DIM), lambda i, c=c: (i, c))
    pblk = lambda c: pl.BlockSpec((8, RWKV_DIM), lambda i, c=c: (jnp.maximum(i * (tm // 8) - 1, 0), c))
    full = lambda a: pl.BlockSpec(a.shape, lambda i: (0,) * a.ndim)
    out = jax.ShapeDtypeStruct((t, RWKV_DIM), F32)
    oblk = pl.BlockSpec((tm, RWKV_DIM), lambda i: (i, 0))
    return pl.pallas_call(
        functools.partial(_rwkv_prep_kernel, tiles_per_seq=seq // tm),
        grid=(t // tm,),
        in_specs=[cblk(U_R // 512), cblk(U_K // 512), cblk(U_V // 512),
                  pl.BlockSpec((tm, LANES), lambda i: (i, U_WA // LANES)),
                  pblk(U_R // 512), pblk(U_K // 512), pblk(U_V // 512),
                  pl.BlockSpec((8, LANES), lambda i: (jnp.maximum(i * (tm // 8) - 1, 0), U_WA // LANES)),
                  full(mu_r), full(mu_k), full(mu_v), full(mu_wa), full(w0), full(w2p),
                  full(a0), full(a2p), full(kkw), full(kaw), full(eones)],
        out_specs=[oblk] * 6,
        out_shape=[out] * 6,
        compiler_params=_cparams(("parallel",)),
        name="rwkv_prep",
    )(u, u, u, u, u, u, u, u, mu_r, mu_k, mu_v, mu_wa, w0, w2p, a0, a2p, kkw, kaw, eones)


def _stack(x, even_lane):
    return jnp.concatenate([jnp.where(even_lane, x, 0.0), jnp.where(even_lane, 0.0, x)], axis=0)


def _unstack(x):
    return x[:CHUNK] + x[CHUNK:]


def _chunk_terms(r, lw, k, v, a, b, masks):
    even_lane, strict, incl, same_blk, eye, tri = masks
    lw_hi = lw.astype(BF16)
    lw_lo = (lw - lw_hi.astype(F32)).astype(BF16)
    cum = (jnp.dot(tri, lw_hi, preferred_element_type=F32)
           + jnp.dot(tri, lw_lo, preferred_element_type=F32))
    cum_end = cum[CHUNK - 1:CHUNK]
    p_inv = jnp.exp(-cum)
    to_end = jnp.exp(cum_end - cum)
    at = a * jnp.exp(cum - lw)
    rt = r * jnp.exp(cum)
    bt = b * p_inv
    kt = k * p_inv
    at_s, rt_s, v_s = _stack(at, even_lane), _stack(rt, even_lane), _stack(v, even_lane)

    lhs = jnp.concatenate([at_s, rt_s], axis=0).astype(BF16)
    rhs = jnp.concatenate([bt, bt, kt, kt], axis=0).astype(BF16)
    g = lax.dot_general(lhs, rhs, (((1,), (1,)), ((), ())), preferred_element_type=F32)
    a_ab = jnp.where(strict, g[:LANES, :LANES], 0.0)
    a_ak = jnp.where(strict, g[:LANES, LANES:], 0.0)
    a_rb = jnp.where(incl, g[LANES:, :LANES], 0.0)
    a_rk = jnp.where(incl, g[LANES:, LANES:], 0.0)

    x = a_ab
    tinv = jnp.where(eye, 1.0, 0.0) + x
    n = 2
    while n < CHUNK:
        x = _bdot(x, x)
        tinv = tinv + _bdot(tinv, x)
        n *= 2

    akv = _bdot(a_ak, v_s)
    wu = _bdot(tinv, jnp.concatenate([at_s, akv], axis=1))
    w_s, u0_s = wu[:, :LANES], wu[:, LANES:]
    zeros = jnp.zeros((LANES, LANES), F32)
    big = jnp.concatenate([wu, jnp.concatenate([zeros, v_s], axis=1)], axis=0)
    top = _bdot(jnp.concatenate([a_rb, a_rk], axis=1), big)
    rw_s = rt_s + top[:, :LANES]
    y0_s = top[:, LANES:]

    bk = jnp.concatenate([b * to_end, k * to_end], axis=0)
    rhs2 = jnp.concatenate(
        [jnp.concatenate([_unstack(w_s), _unstack(u0_s)], axis=1),
         jnp.concatenate([jnp.zeros((CHUNK, LANES), F32), v], axis=1)], axis=0)
    mz = lax.dot_general(bk.astype(BF16), rhs2.astype(BF16), (((0,), (0,)), ((), ())),
                         preferred_element_type=F32)
    m = jnp.where(same_blk, mz[:, :LANES], 0.0) + jnp.where(eye, jnp.exp(cum_end), 0.0)
    z0 = jnp.where(same_blk, mz[:, LANES:], 0.0)
    return rw_s, y0_s, m, z0


def _rwkv_scan_kernel(r_ref, lw_ref, k_ref, v_ref, a_ref, b_ref, g_ref, gng, gnb, rkw, eones,
                      o_ref, state_ref, y_ref, *, tt):
    @pl.when(pl.program_id(1) == 0)
    def _():
        state_ref[...] = jnp.zeros(state_ref.shape, F32)

    rr = lax.broadcasted_iota(jnp.int32, (LANES, LANES), 0)
    cc = lax.broadcasted_iota(jnp.int32, (LANES, LANES), 1)
    same_blk = (rr // CHUNK) == (cc // CHUNK)
    strict = same_blk & ((cc % CHUNK) < (rr % CHUNK))
    incl = same_blk & ((cc % CHUNK) <= (rr % CHUNK))
    eye = rr == cc
    even_lane = lax.broadcasted_iota(jnp.int32, (CHUNK, LANES), 1) < CHUNK
    tr = lax.broadcasted_iota(jnp.int32, (CHUNK, CHUNK), 0)
    tc = lax.broadcasted_iota(jnp.int32, (CHUNK, CHUNK), 1)
    tri = jnp.where(tc <= tr, 1.0, 0.0).astype(BF16)
    masks = (even_lane, strict, incl, same_blk, eye, tri)

    def chunk_body(c, carry):
        rows = pl.ds(pl.multiple_of(c * CHUNK, CHUNK), CHUNK)
        for p in range(RWKV_DIM // LANES):
            cols = slice(p * LANES, (p + 1) * LANES)
            rw_s, y0_s, m, z0 = _chunk_terms(
                r_ref[rows, cols], lw_ref[rows, cols], k_ref[rows, cols],
                v_ref[rows, cols], a_ref[rows, cols], b_ref[rows, cols], masks)
            st = state_ref[p]
            ys = _bdot(jnp.concatenate([rw_s, m], axis=0), st)
            y_ref[rows, cols] = _unstack(ys[:LANES] + y0_s)
            state_ref[p] = ys[LANES:] + z0
        return carry

    lax.fori_loop(0, tt // CHUNK, chunk_body, 0)

    e = eones[...]

    def headsum(x):
        hi = x.astype(BF16)
        lo = (x - hi.astype(F32)).astype(BF16)
        return (jnp.dot(hi, e, preferred_element_type=F32)
                + jnp.dot(lo, e, preferred_element_type=F32))

    y = y_ref[...]
    d = y - headsum(y) * (1.0 / RWKV_N)
    var = headsum(d * d) * (1.0 / RWKV_N)
    yn = d * lax.rsqrt(var + RWKV_GN_EPS) * gng[...] + gnb[...]
    yn = yn + headsum(r_ref[...] * k_ref[...] * rkw[...]) * v_ref[...]
    g = g_ref[...]
    o_ref[...] = (yn * (g * jax.nn.sigmoid(g))).astype(o_ref.dtype)


def _rwkv_scan(r, lw, k, v, a, b, u, gng, gnb, rkw, eones, batch, seq, tt):
    t = batch * seq
    nt = seq // tt
    blk = pl.BlockSpec((tt, RWKV_DIM), lambda bi, i: (bi * nt + i, 0))
    full = lambda x: pl.BlockSpec(x.shape, lambda bi, i: (0,) * x.ndim)
    return pl.pallas_call(
        functools.partial(_rwkv_scan_kernel, tt=tt),
        grid=(batch, nt),
        in_specs=[blk] * 6 + [pl.BlockSpec((tt, RWKV_DIM), lambda bi, i: (bi * nt + i, U_RG // RWKV_DIM)),
                              full(gng), full(gnb), full(rkw), full(eones)],
        out_specs=blk,
        out_shape=jax.ShapeDtypeStruct((t, RWKV_DIM), BF16),
        scratch_shapes=[pltpu.VMEM((RWKV_DIM // LANES, LANES, LANES), F32),
                        pltpu.VMEM((tt, RWKV_DIM), F32)],
        compiler_params=_cparams(("parallel", "arbitrary")),
        name="rwkv_scan",
    )(r, lw, k, v, a, b, u, gng, gnb, rkw, eones)


def _outproj_kernel(yc, ym, yr, x_ref, w1, w2, w3, lg, lb, o_ref):
    acc = jnp.dot(yc[...], w1[...], preferred_element_type=F32)
    acc = acc + jnp.dot(ym[...], w2[...], preferred_element_type=F32)
    acc = acc + jnp.dot(yr[...], w3[...], preferred_element_type=F32)
    z = DEEPNORM_ALPHA * x_ref[...] + acc
    mu = jnp.mean(z, -1, keepdims=True)
    d = z - mu
    var = jnp.mean(d * d, -1, keepdims=True)
    o_ref[...] = d * lax.rsqrt(var + LN_EPS) * lg[...] + lb[...]


def _outproj(yc, ym, yr, x2, w1, w2, w3, lg, lb, tm):
    t = x2.shape[0]
    rblk = lambda d: pl.BlockSpec((tm, d), lambda i: (i, 0))
    full = lambda a: pl.BlockSpec(a.shape, lambda i: (0,) * a.ndim)
    return pl.pallas_call(
        _outproj_kernel,
        grid=(t // tm,),
        in_specs=[rblk(CONV_DIM), rblk(MLA_DIM), rblk(RWKV_DIM), rblk(D_MODEL),
                  full(w1), full(w2), full(w3), full(lg), full(lb)],
        out_specs=rblk(D_MODEL),
        out_shape=jax.ShapeDtypeStruct((t, D_MODEL), F32),
        compiler_params=_cparams(("parallel",)),
        name="outproj_ln",
    )(yc, ym, yr, x2, w1, w2, w3, lg, lb)


def _rot_half_cols(w):
    return jnp.concatenate([-w[:, QK_ROPE // 2:], w[:, :QK_ROPE // 2]], axis=1)


def _pack_w_in(w):
    o_ckv, o_kpe, o_mg, o_rc = 2560, 2816, 2880, 3904
    ckv = w[:, o_ckv:o_kpe]
    kpe = w[:, o_kpe:o_mg]
    mg = w[:, o_mg:o_rc]
    r = w[:, o_rc:o_rc + 512]
    wd = w[:, o_rc + 512:o_rc + 576]
    k = w[:, o_rc + 576:o_rc + 1088]
    v = w[:, o_rc + 1088:o_rc + 1600]
    ad = w[:, o_rc + 1600:o_rc + 1664]
    rg = w[:, o_rc + 1664:]
    return jnp.concatenate([w[:, :o_ckv], r, k, v, rg, mg, ckv, kpe, _rot_half_cols(kpe),
                            wd, ad], axis=1).astype(BF16)


def _pack_mu(mu):
    r, wd, k, v, ad = (mu[0:512], mu[512:576], mu[576:1088], mu[1088:1600], mu[1600:1664])
    row = lambda a: a.reshape(1, -1)
    return row(r), row(k), row(v), row(jnp.concatenate([wd, ad]))


def _pack_w_uq(w):
    cols = []
    for h in range(MLA_HEADS):
        wh = w[:, h * QK_HEAD:(h + 1) * QK_HEAD]
        pe = wh[:, QK_NOPE:]
        cols += [wh[:, :QK_NOPE], pe, _rot_half_cols(pe)]
    return jnp.concatenate(cols, axis=1).astype(BF16)


def _tile(n, pref):
    return pref if n % pref == 0 else n


def kernel(x, positions, w_in, conv_w, q_norm_g, w_uq, kv_norm_g, w_ukv, rwkv_mu, rwkv_w0,
           rwkv_w2, rwkv_a0, rwkv_a2, rwkv_k_k, rwkv_k_a, rwkv_r_k, rwkv_gn_g, rwkv_gn_b,
           w_out, ln_g, ln_b):
    batch, seq, _ = x.shape
    t = batch * seq
    row = lambda a: a.reshape(1, -1)

    inv_freq = ROPE_THETA ** (-jnp.arange(0, QK_ROPE, 2, dtype=F32) / QK_ROPE)
    invf4 = jnp.tile(inv_freq, 4).reshape(1, LANES)
    cs = _rope_table(positions.reshape(t, 1), invf4, _tile(t, 1024))

    head_of = jnp.arange(RWKV_DIM) // RWKV_N
    eones = (head_of[:, None] == head_of[None, :]).astype(BF16)
    zpad = jnp.zeros((DECAY_LORA, RWKV_DIM), F32)

    x2 = x.reshape(t, D_MODEL)
    for l in range(DEPTH):
        u = _inproj(x2, _pack_w_in(w_in[l]), _tile(t, 512), 768)
        y_conv = _conv_branch(u, conv_w[l], seq, _tile(seq, 512))
        q, k, v = _mla_prep(u, cs, row(q_norm_g[l]), _pack_w_uq(w_uq[l]), row(kv_norm_g[l]),
                            w_ukv[l].astype(BF16), batch, seq, _tile(seq, 512))
        y_mla = _attention(q, k, v, u, seq, _tile(seq, 1024), 256)
        mu_r, mu_k, mu_v, mu_wa = _pack_mu(rwkv_mu[l])
        w2p = jnp.concatenate([rwkv_w2[l], zpad], axis=0).astype(BF16)
        a2p = jnp.concatenate([zpad, rwkv_a2[l]], axis=0).astype(BF16)
        rr, lw, kk, vv, aa, bb = _rwkv_prep(
            u, mu_r, mu_k, mu_v, mu_wa, row(rwkv_w0[l]), w2p, row(rwkv_a0[l]), a2p,
            row(rwkv_k_k[l]), row(rwkv_k_a[l]), eones, seq, _tile(seq, 512))
        y_rwkv = _rwkv_scan(rr, lw, kk, vv, aa, bb, u, row(rwkv_gn_g[l]), row(rwkv_gn_b[l]),
                            row(rwkv_r_k[l]), eones, batch, seq, _tile(seq, 512))
        wo = w_out[l].astype(BF16)
        x2 = _outproj(y_conv, y_mla, y_rwkv, x2,
                      wo[:CONV_DIM], wo[CONV_DIM:CONV_DIM + MLA_DIM], wo[CONV_DIM + MLA_DIM:],
                      row(ln_g[l]), row(ln_b[l]), _tile(t, 256))
    return x2.reshape(batch, seq, D_MODEL)
```

```python
import functools

import jax
import jax.numpy as jnp
from jax import lax
from jax.experimental import pallas as pl
from jax.experimental.pallas import tpu as pltpu

F32 = jnp.float32
BF16 = jnp.bfloat16

D_MODEL = 2048
CONV_DIM = 512
CONV_K = 3
MLA_HEADS = 8
QK_NOPE = 128
QK_ROPE = 64
QK_HEAD = QK_NOPE + QK_ROPE
V_DIM = 128
MLA_DIM = MLA_HEADS * V_DIM
Q_LORA = 512
KV_LORA = 256
ROPE_THETA = 10000.0
RWKV_HEADS = 8
RWKV_N = 64
RWKV_DIM = RWKV_HEADS * RWKV_N
DECAY_LORA = 64
A_LORA = 64
RWKV_GN_EPS = 64e-5
LN_EPS = 1e-5
RMS_EPS = 1e-6
DEPTH = 2
DEEPNORM_ALPHA = (2 * DEPTH) ** 0.25

U_CB, U_CC, U_CH, U_CG = 0, 512, 1024, 1536
U_CQ = 2048
U_R, U_K, U_V, U_RG = 2560, 3072, 3584, 4096
U_MG = 4608
U_CKV = 5632
U_KPE = 5888
U_WA = 6016
U_TOTAL = 6144

LANES = 128
CHUNK = 64
VMEM_LIMIT = 56 * 1024 * 1024

LOG2E = 1.4426950408889634


def _cparams(sem):
    return pltpu.CompilerParams(dimension_semantics=sem, vmem_limit_bytes=VMEM_LIMIT)


def _bdot(a, b):
    return jnp.dot(a.astype(BF16), b.astype(BF16), preferred_element_type=F32)


def _inproj_kernel(x_ref, w_ref, o_ref, xb_ref):
    @pl.when(pl.program_id(1) == 0)
    def _():
        xb_ref[...] = x_ref[...].astype(BF16)

    o_ref[...] = jnp.dot(xb_ref[...], w_ref[...], preferred_element_type=F32)


def _inproj(x2, w_p, tm, tn):
    t = x2.shape[0]
    return pl.pallas_call(
        _inproj_kernel,
        grid=(t // tm, U_TOTAL // tn),
        in_specs=[pl.BlockSpec((tm, D_MODEL), lambda i, j: (i, 0)),
                  pl.BlockSpec((D_MODEL, tn), lambda i, j: (0, j))],
        out_specs=pl.BlockSpec((tm, tn), lambda i, j: (i, j)),
        out_shape=jax.ShapeDtypeStruct((t, U_TOTAL), F32),
        scratch_shapes=[pltpu.VMEM((tm, D_MODEL), BF16)],
        compiler_params=_cparams(("parallel", "arbitrary")),
        name="inproj",
    )(x2, w_p)


def _shift_rows(x, prev, n, row):
    out = pltpu.roll(x, n, 0)
    for r in range(n):
        out = jnp.where(row == r, prev[8 - n + r:8 - n + r + 1], out)
    return out


def _conv_kernel(cb, cc, ch, cg, ccp, chp, w, o, *, tiles_per_seq):
    first = (pl.program_id(0) % tiles_per_seq) == 0
    up = cc[...] * ch[...]
    prev = jnp.where(first, 0.0, ccp[...] * chp[...])
    row = lax.broadcasted_iota(jnp.int32, up.shape, 0)
    u1 = _shift_rows(up, prev, 1, row)
    u2 = _shift_rows(up, prev, 2, row)
    wv = w[...]
    y = wv[0:1] * u2 + wv[1:2] * u1 + wv[2:3] * up
    g = cg[...]
    o[...] = (cb[...] * y * (g * jax.nn.sigmoid(g))).astype(o.dtype)


def _conv_branch(u, conv_w, seq, tm):
    t = u.shape[0]
    cblk = lambda c: pl.BlockSpec((tm, CONV_DIM), lambda i, c=c: (i, c))
    pblk = lambda c: pl.BlockSpec((8, CONV_DIM), lambda i, c=c: (jnp.maximum(i * (tm // 8) - 1, 0), c))
    return pl.pallas_call(
        functools.partial(_conv_kernel, tiles_per_seq=seq // tm),
        grid=(t // tm,),
        in_specs=[cblk(U_CB // 512), cblk(U_CC // 512), cblk(U_CH // 512), cblk(U_CG // 512),
                  pblk(U_CC // 512), pblk(U_CH // 512),
                  pl.BlockSpec((CONV_K, CONV_DIM), lambda i: (0, 0))],
        out_specs=pl.BlockSpec((tm, CONV_DIM), lambda i: (i, 0)),
        out_shape=jax.ShapeDtypeStruct((t, CONV_DIM), BF16),
        compiler_params=_cparams(("parallel",)),
        name="conv_branch",
    )(u, u, u, u, u, u, conv_w)


def _rope_kernel(pos_ref, invf_ref, cs_ref):
    ang = pos_ref[...].astype(F32) * invf_ref[...]
    lane = lax.broadcasted_iota(jnp.int32, ang.shape, 1)
    cs_ref[...] = jnp.where(lane < QK_ROPE, jnp.cos(ang), jnp.sin(ang))


def _rope_table(pos2, invf4, tm):
    t = pos2.shape[0]
    return pl.pallas_call(
        _rope_kernel,
        grid=(t // tm,),
        in_specs=[pl.BlockSpec((tm, 1), lambda i: (i, 0)),
                  pl.BlockSpec((1, LANES), lambda i: (0, 0))],
        out_specs=pl.BlockSpec((tm, LANES), lambda i: (i, 0)),
        out_shape=jax.ShapeDtypeStruct((t, LANES), F32),
        compiler_params=_cparams(("parallel",)),
        name="rope_table",
    )(pos2, invf4)


def _rms(x, g):
    return x * lax.rsqrt(jnp.mean(x * x, -1, keepdims=True) + RMS_EPS) * g


def _mla_prep_kernel(cq_ref, ckv_ref, kp_ref, cs_ref, qg_ref, wq_ref, kg_ref, wkv_ref,
                     q_out, k_out, v_out, *, qscale):
    cs = cs_ref[...]
    qf = _bdot(_rms(cq_ref[...], qg_ref[...]), wq_ref[...])
    kvf = _bdot(_rms(ckv_ref[...], kg_ref[...]), wkv_ref[...])
    kp = kp_ref[...] * cs
    kpe = (kp + pltpu.roll(kp, QK_ROPE, 1))[:, :QK_ROPE].astype(BF16)
    for h in range(MLA_HEADS):
        c0 = 2 * LANES * h
        pp = qf[:, c0 + LANES:c0 + 2 * LANES] * cs
        pe = pp + pltpu.roll(pp, QK_ROPE, 1)
        q_out[h, :, 0:QK_NOPE] = (qf[:, c0:c0 + LANES] * qscale).astype(BF16)
        q_out[h, :, QK_NOPE:QK_HEAD] = (pe[:, :QK_ROPE] * qscale).astype(BF16)
        k_out[h, :, 0:QK_NOPE] = kvf[:, c0:c0 + LANES].astype(BF16)
        k_out[h, :, QK_NOPE:QK_HEAD] = kpe
        v_out[h, :, 0:V_DIM] = kvf[:, c0 + LANES:c0 + 2 * LANES].astype(BF16)
        v_out[h, :, V_DIM:2 * V_DIM] = jnp.ones((kvf.shape[0], V_DIM), BF16)


def _mla_prep(u, cs, qg, wq_p, kg, wkv, batch, seq, tm):
    t = u.shape[0]
    nt = seq // tm
    hblk = lambda d: pl.BlockSpec((None, MLA_HEADS, tm, d), lambda i: (i // nt, 0, i % nt, 0))
    full = lambda a: pl.BlockSpec(a.shape, lambda i: (0,) * a.ndim)
    qscale = QK_HEAD ** -0.5 * LOG2E
    return pl.pallas_call(
        functools.partial(_mla_prep_kernel, qscale=qscale),
        grid=(t // tm,),
        in_specs=[pl.BlockSpec((tm, Q_LORA), lambda i: (i, U_CQ // Q_LORA)),
                  pl.BlockSpec((tm, KV_LORA), lambda i: (i, U_CKV // KV_LORA)),
                  pl.BlockSpec((tm, LANES), lambda i: (i, U_KPE // LANES)),
                  pl.BlockSpec((tm, LANES), lambda i: (i, 0)),
                  full(qg), full(wq_p), full(kg), full(wkv)],
        out_specs=[hblk(QK_HEAD), hblk(QK_HEAD), hblk(2 * V_DIM)],
        out_shape=[jax.ShapeDtypeStruct((batch, MLA_HEADS, seq, QK_HEAD), BF16),
                   jax.ShapeDtypeStruct((batch, MLA_HEADS, seq, QK_HEAD), BF16),
                   jax.ShapeDtypeStruct((batch, MLA_HEADS, seq, 2 * V_DIM), BF16)],
        compiler_params=_cparams(("parallel",)),
        name="mla_prep",
    )(u, u, u, cs, qg, wq_p, kg, wkv)


def _attn_kernel(q_ref, k_ref, v_ref, g_ref, o_ref, m_ref, acc_ref, s0, s1, p0, p1, a0, a1, *, tq, sub):
    tk = tq // 2
    i = pl.program_id(2)
    s_scr, p_scr, a_scr = (s0, s1), (p0, p1), (a0, a1)
    lo, hi, full = slice(0, tk), slice(tk, tq), slice(0, tq)
    nt = (((1,), (1,)), ((), ()))

    m_ref[...] = jnp.full(m_ref.shape, -jnp.inf, F32)
    acc_ref[...] = jnp.zeros(acc_ref.shape, F32)
    p1[...] = jnp.zeros(p1.shape, BF16)
    a1[...] = jnp.ones(a1.shape, F32)

    def qk(t, slot, rows):
        kb = k_ref[pl.ds(pl.multiple_of(t * tk, tk), tk), :]
        s_scr[slot][rows, :] = lax.dot_general(q_ref[rows, :], kb, nt, preferred_element_type=F32)

    def softmax(slot, rows, causal):
        for r0 in range(rows.start, rows.stop, sub):
            rs = slice(r0, r0 + sub)
            s = s_scr[slot][rs, :]
            if causal:
                r = lax.broadcasted_iota(jnp.int32, s.shape, 0) + (r0 - rows.start)
                c = lax.broadcasted_iota(jnp.int32, s.shape, 1)
                s = jnp.where(c <= r, s, -jnp.inf)
            m_prev = m_ref[rs, :]
            m_new = jnp.maximum(m_prev, jnp.max(s, axis=1, keepdims=True))
            p_scr[slot][rs, :] = jnp.exp2(s - pltpu.repeat(m_new, tk // LANES, 1)).astype(BF16)
            a_scr[slot][rs, :] = jnp.exp2(m_prev - m_new)
            m_ref[rs, :] = m_new

    def pv(t, slot, rows):
        vb = v_ref[pl.ds(pl.multiple_of(t * tk, tk), tk), :]
        acc_ref[rows, :] = (pltpu.repeat(a_scr[slot][rows, :], 2, 1) * acc_ref[rows, :]
                            + jnp.dot(p_scr[slot][rows, :], vb, preferred_element_type=F32))

    def step(t, slot):
        qk(t + 1, 1 - slot, full)
        pv(jnp.maximum(t - 1, 0), 1 - slot, full)
        softmax(slot, full, False)

    def pair(tt, carry):
        step(2 * tt, 0)
        step(2 * tt + 1, 1)
        return carry

    qk(0, 0, full)
    lax.fori_loop(0, i, pair, 0)
    d = 2 * i
    qk(d + 1, 1, hi)
    pv(jnp.maximum(d - 1, 0), 1, full)
    softmax(0, lo, True)
    softmax(0, hi, False)
    softmax(1, hi, True)
    pv(d, 0, full)
    pv(d + 1, 1, hi)
    g = g_ref[...]
    acc = acc_ref[...]
    o_ref[...] = (acc[:, :V_DIM] / acc[:, V_DIM:] * (g * jax.nn.sigmoid(g))).astype(o_ref.dtype)


def _attention(q, k, v_aug, u, seq, tq, sub):
    batch = q.shape[0]
    t = batch * seq
    nq = seq // tq
    return pl.pallas_call(
        functools.partial(_attn_kernel, tq=tq, sub=sub),
        grid=(batch, MLA_HEADS, nq),
        in_specs=[pl.BlockSpec((None, None, tq, QK_HEAD), lambda b, h, i: (b, h, i, 0)),
                  pl.BlockSpec((None, None, seq, QK_HEAD), lambda b, h, i: (b, h, 0, 0)),
                  pl.BlockSpec((None, None, seq, 2 * V_DIM), lambda b, h, i: (b, h, 0, 0)),
                  pl.BlockSpec((tq, V_DIM), lambda b, h, i: (b * nq + i, U_MG // V_DIM + h))],
        out_specs=pl.BlockSpec((tq, V_DIM), lambda b, h, i: (b * nq + i, h)),
        out_shape=jax.ShapeDtypeStruct((t, MLA_DIM), BF16),
        scratch_shapes=[pltpu.VMEM((tq, LANES), F32), pltpu.VMEM((tq, 2 * V_DIM), F32),
                        pltpu.VMEM((tq, tq // 2), F32), pltpu.VMEM((tq, tq // 2), F32),
                        pltpu.VMEM((tq, tq // 2), BF16), pltpu.VMEM((tq, tq // 2), BF16),
                        pltpu.VMEM((tq, LANES), F32), pltpu.VMEM((tq, LANES), F32)],
        compiler_params=_cparams(("parallel", "parallel", "arbitrary")),
        name="mla_attention",
    )(q, k, v_aug, u)


def _softplus(z):
    return jnp.maximum(z, 0.0) + jnp.log1p(jnp.exp(-jnp.abs(z)))


def _rwkv_prep_kernel(r_ref, k_ref, v_ref, wa_ref, rp_ref, kp_ref, vp_ref, wap_ref,
                      mu_r, mu_k, mu_v, mu_wa, w0, w2p, a0, a2p, kkw, kaw, eones,
                      r_o, lw_o, k_o, v_o, a_o, b_o, *, tiles_per_seq):
    first = (pl.program_id(0) % tiles_per_seq) == 0
    row = lax.broadcasted_iota(jnp.int32, r_ref.shape, 0)
    row_wa = lax.broadcasted_iota(jnp.int32, wa_ref.shape, 0)

    def lerp(x_ref, p_ref, mu, rw):
        x = x_ref[...]
        prev = jnp.where(first, 0.0, p_ref[...])
        return x + (_shift_rows(x, prev, 1, rw) - x) * mu[...]

    r = lerp(r_ref, rp_ref, mu_r, row)
    k = lerp(k_ref, kp_ref, mu_k, row)
    v = lerp(v_ref, vp_ref, mu_v, row)
    wa = lerp(wa_ref, wap_ref, mu_wa, row_wa)
    w = -_softplus(-(w0[...] + _bdot(jnp.tanh(wa), w2p[...]))) - 0.5
    a_sig = jax.nn.sigmoid(a0[...] + _bdot(wa, a2p[...]))
    kk = k * kkw[...]
    ss = _bdot(kk * kk, eones[...])
    kk = kk * (1.0 / jnp.maximum(jnp.sqrt(ss), 1e-12))
    r_o[...] = r
    lw_o[...] = -jnp.exp(w)
    k_o[...] = k * (1.0 + (a_sig - 1.0) * kaw[...])
    v_o[...] = v
    a_o[...] = -kk
    b_o[...] = kk * a_sig


def _rwkv_prep(u, mu_r, mu_k, mu_v, mu_wa, w0, w2p, a0, a2p, kkw, kaw, eones, seq, tm):
    t = u.shape[0]
    cblk = lambda c: pl.BlockSpec((tm, RWKV_DIM), lambda i, c=c: (i, c))
    pblk = lambda c: pl.BlockSpec((8, RWKV_DIM), lambda i, c=c: (jnp.maximum(i * (tm // 8) - 1, 0), c))
    full = lambda a: pl.BlockSpec(a.shape, lambda i: (0,) * a.ndim)
    out = jax.ShapeDtypeStruct((t, RWKV_DIM), F32)
    oblk = pl.BlockSpec((tm, RWKV_DIM), lambda i: (i, 0))
    return pl.pallas_call(
        functools.partial(_rwkv_prep_kernel, tiles_per_seq=seq // tm),
        grid=(t // tm,),
        in_specs=[cblk(U_R // 512), cblk(U_K // 512), cblk(U_V // 512),
                  pl.BlockSpec((tm, LANES), lambda i: (i, U_WA // LANES)),
                  pblk(U_R // 512), pblk(U_K // 512), pblk(U_V // 512),
                  pl.BlockSpec((8, LANES), lambda i: (jnp.maximum(i * (tm // 8) - 1, 0), U_WA // LANES)),
                  full(mu_r), full(mu_k), full(mu_v), full(mu_wa), full(w0), full(w2p),
                  full(a0), full(a2p), full(kkw), full(kaw), full(eones)],
        out_specs=[oblk] * 6,
        out_shape=[out] * 6,
        compiler_params=_cparams(("parallel",)),
        name="rwkv_prep",
    )(u, u, u, u, u, u, u, u, mu_r, mu_k, mu_v, mu_wa, w0, w2p, a0, a2p, kkw, kaw, eones)


def _stack(x, even_lane):
    return jnp.concatenate([jnp.where(even_lane, x, 0.0), jnp.where(even_lane, 0.0, x)], axis=0)


def _unstack(x):
    return x[:CHUNK] + x[CHUNK:]


def _each(fn, *lists):
    return [fn(*args) for args in zip(*lists)]


def _chunk_terms(r, lw, k, v, a, b, masks):
    even_lane, strict, incl, same_blk, eye, tri = masks
    nt = (((1,), (1,)), ((), ()))
    tn = (((0,), (0,)), ((), ()))

    def cumsum(x):
        hi = x.astype(BF16)
        lo = (x - hi.astype(F32)).astype(BF16)
        return (jnp.dot(tri, hi, preferred_element_type=F32)
                + jnp.dot(tri, lo, preferred_element_type=F32))

    cum = _each(cumsum, lw)
    cum_end = _each(lambda c: c[CHUNK - 1:CHUNK], cum)
    at_s = _each(lambda x, c, w: _stack(x * jnp.exp(c - w), even_lane), a, cum, lw)
    rt_s = _each(lambda x, c: _stack(x * jnp.exp(c), even_lane), r, cum)
    v_s = _each(lambda x: _stack(x, even_lane), v)
    bt = _each(lambda x, c: (x * jnp.exp(-c)).astype(BF16), b, cum)
    kt = _each(lambda x, c: (x * jnp.exp(-c)).astype(BF16), k, cum)
    g = _each(lambda x, y, p, q: lax.dot_general(
        jnp.concatenate([x, y], axis=0).astype(BF16), jnp.concatenate([p, p, q, q], axis=0), nt,
        preferred_element_type=F32), at_s, rt_s, bt, kt)
    x = _each(lambda m: jnp.where(strict, m[:LANES, :LANES], 0.0), g)
    a_ak = _each(lambda m: jnp.where(strict, m[:LANES, LANES:], 0.0), g)
    a_r = _each(lambda m: jnp.concatenate([jnp.where(incl, m[LANES:, :LANES], 0.0),
                                           jnp.where(incl, m[LANES:, LANES:], 0.0)], axis=1), g)
    akv = _each(_bdot, a_ak, v_s)

    tinv = _each(lambda m: jnp.where(eye, 1.0, 0.0) + m, x)
    n = 2
    while n < CHUNK:
        x = _each(lambda m: _bdot(m, m), x)
        tinv = _each(lambda t, m: t + _bdot(t, m), tinv, x)
        n *= 2

    wu = _each(lambda t, p, q: _bdot(t, jnp.concatenate([p, q], axis=1)), tinv, at_s, akv)
    zeros = jnp.zeros((LANES, LANES), F32)
    top = _each(lambda m, w, q: _bdot(m, jnp.concatenate(
        [w, jnp.concatenate([zeros, q], axis=1)], axis=0)), a_r, wu, v_s)
    rw_s = _each(lambda x, t: x + t[:, :LANES], rt_s, top)
    y0_s = _each(lambda t: t[:, LANES:], top)

    def carry_terms(bb, kk, vv, c, ce, w):
        to_end = jnp.exp(ce - c)
        bk = jnp.concatenate([bb * to_end, kk * to_end], axis=0)
        rhs = jnp.concatenate(
            [jnp.concatenate([_unstack(w[:, :LANES]), _unstack(w[:, LANES:])], axis=1),
             jnp.concatenate([jnp.zeros((CHUNK, LANES), F32), vv], axis=1)], axis=0)
        mz = lax.dot_general(bk.astype(BF16), rhs.astype(BF16), tn, preferred_element_type=F32)
        m = jnp.where(same_blk, mz[:, :LANES], 0.0) + jnp.where(eye, jnp.exp(ce), 0.0)
        return m, jnp.where(same_blk, mz[:, LANES:], 0.0)

    mz = _each(carry_terms, b, k, v, cum, cum_end, wu)
    return rw_s, y0_s, [t[0] for t in mz], [t[1] for t in mz]


def _rwkv_scan_kernel(r_ref, lw_ref, k_ref, v_ref, a_ref, b_ref, g_ref, gng, gnb, rkw, eones,
                      o_ref, state_ref, y_ref, *, tt):
    @pl.when(pl.program_id(0) == 0)
    def _():
        state_ref[...] = jnp.zeros(state_ref.shape, F32)

    rr = lax.broadcasted_iota(jnp.int32, (LANES, LANES), 0)
    cc = lax.broadcasted_iota(jnp.int32, (LANES, LANES), 1)
    same_blk = (rr // CHUNK) == (cc // CHUNK)
    strict = same_blk & ((cc % CHUNK) < (rr % CHUNK))
    incl = same_blk & ((cc % CHUNK) <= (rr % CHUNK))
    eye = rr == cc
    even_lane = lax.broadcasted_iota(jnp.int32, (CHUNK, LANES), 1) < CHUNK
    tr = lax.broadcasted_iota(jnp.int32, (CHUNK, CHUNK), 0)
    tc = lax.broadcasted_iota(jnp.int32, (CHUNK, CHUNK), 1)
    tri = jnp.where(tc <= tr, 1.0, 0.0).astype(BF16)
    masks = (even_lane, strict, incl, same_blk, eye, tri)

    batch = r_ref.shape[0]
    chains = [(bi, slice(p * LANES, (p + 1) * LANES))
              for bi in range(batch) for p in range(RWKV_DIM // LANES)]

    def chunk_body(c, carry):
        rows = pl.ds(pl.multiple_of(c * CHUNK, CHUNK), CHUNK)
        load = lambda ref: [ref[bi, rows, cs] for bi, cs in chains]
        rw_s, y0_s, m, z0 = _chunk_terms(load(r_ref), load(lw_ref), load(k_ref), load(v_ref),
                                         load(a_ref), load(b_ref), masks)
        ys = [_bdot(jnp.concatenate([rw_s[n], m[n]], axis=0), state_ref[n])
              for n in range(len(chains))]
        for n, (bi, cs) in enumerate(chains):
            y_ref[bi, rows, cs] = _unstack(ys[n][:LANES] + y0_s[n])
            state_ref[n] = ys[n][LANES:] + z0[n]
        return carry

    lax.fori_loop(0, tt // CHUNK, chunk_body, 0)

    e = eones[...]

    def headsum(x):
        hi = x.astype(BF16)
        lo = (x - hi.astype(F32)).astype(BF16)
        return (jnp.dot(hi, e, preferred_element_type=F32)
                + jnp.dot(lo, e, preferred_element_type=F32))

    for bi in range(batch):
        y = y_ref[bi]
        d = y - headsum(y) * (1.0 / RWKV_N)
        var = headsum(d * d) * (1.0 / RWKV_N)
        yn = d * lax.rsqrt(var + RWKV_GN_EPS) * gng[...] + gnb[...]
        yn = yn + headsum(r_ref[bi] * k_ref[bi] * rkw[...]) * v_ref[bi]
        g = g_ref[bi]
        o_ref[bi] = (yn * (g * jax.nn.sigmoid(g))).astype(o_ref.dtype)


def _rwkv_scan(r, lw, k, v, a, b, u, gng, gnb, rkw, eones, batch, seq, tt):
    to3 = lambda x: x.reshape(batch, seq, x.shape[-1])
    blk = pl.BlockSpec((batch, tt, RWKV_DIM), lambda i: (0, i, 0))
    full = lambda x: pl.BlockSpec(x.shape, lambda i: (0,) * x.ndim)
    out = pl.pallas_call(
        functools.partial(_rwkv_scan_kernel, tt=tt),
        grid=(seq // tt,),
        in_specs=[blk] * 6 + [pl.BlockSpec((batch, tt, RWKV_DIM), lambda i: (0, i, U_RG // RWKV_DIM)),
                              full(gng), full(gnb), full(rkw), full(eones)],
        out_specs=blk,
        out_shape=jax.ShapeDtypeStruct((batch, seq, RWKV_DIM), BF16),
        scratch_shapes=[pltpu.VMEM((batch * RWKV_DIM // LANES, LANES, LANES), F32),
                        pltpu.VMEM((batch, tt, RWKV_DIM), F32)],
        compiler_params=_cparams(("arbitrary",)),
        name="rwkv_scan",
    )(to3(r), to3(lw), to3(k), to3(v), to3(a), to3(b), to3(u), gng, gnb, rkw, eones)
    return out.reshape(batch * seq, RWKV_DIM)


def _outproj_kernel(yc, ym, yr, x_ref, w1, w2, w3, lg, lb, o_ref):
    acc = jnp.dot(yc[...], w1[...], preferred_element_type=F32)
    acc = acc + jnp.dot(ym[...], w2[...], preferred_element_type=F32)
    acc = acc + jnp.dot(yr[...], w3[...], preferred_element_type=F32)
    z = DEEPNORM_ALPHA * x_ref[...] + acc
    mu = jnp.mean(z, -1, keepdims=True)
    d = z - mu
    var = jnp.mean(d * d, -1, keepdims=True)
    o_ref[...] = d * lax.rsqrt(var + LN_EPS) * lg[...] + lb[...]


def _outproj(yc, ym, yr, x2, w1, w2, w3, lg, lb, tm):
    t = x2.shape[0]
    rblk = lambda d: pl.BlockSpec((tm, d), lambda i: (i, 0))
    full = lambda a: pl.BlockSpec(a.shape, lambda i: (0,) * a.ndim)
    return pl.pallas_call(
        _outproj_kernel,
        grid=(t // tm,),
        in_specs=[rblk(CONV_DIM), rblk(MLA_DIM), rblk(RWKV_DIM), rblk(D_MODEL),
                  full(w1), full(w2), full(w3), full(lg), full(lb)],
        out_specs=rblk(D_MODEL),
        out_shape=jax.ShapeDtypeStruct((t, D_MODEL), F32),
        compiler_params=_cparams(("parallel",)),
        name="outproj_ln",
    )(yc, ym, yr, x2, w1, w2, w3, lg, lb)


def _rot_half_cols(w):
    return jnp.concatenate([-w[:, QK_ROPE // 2:], w[:, :QK_ROPE // 2]], axis=1)


def _pack_w_in(w):
    o_ckv, o_kpe, o_mg, o_rc = 2560, 2816, 2880, 3904
    ckv = w[:, o_ckv:o_kpe]
    kpe = w[:, o_kpe:o_mg]
    mg = w[:, o_mg:o_rc]
    r = w[:, o_rc:o_rc + 512]
    wd = w[:, o_rc + 512:o_rc + 576]
    k = w[:, o_rc + 576:o_rc + 1088]
    v = w[:, o_rc + 1088:o_rc + 1600]
    ad = w[:, o_rc + 1600:o_rc + 1664]
    rg = w[:, o_rc + 1664:]
    return jnp.concatenate([w[:, :o_ckv], r, k, v, rg, mg, ckv, kpe, _rot_half_cols(kpe),
                            wd, ad], axis=1).astype(BF16)


def _pack_mu(mu):
    r, wd, k, v, ad = (mu[0:512], mu[512:576], mu[576:1088], mu[1088:1600], mu[1600:1664])
    row = lambda a: a.reshape(1, -1)
    return row(r), row(k), row(v), row(jnp.concatenate([wd, ad]))


def _pack_w_uq(w):
    cols = []
    for h in range(MLA_HEADS):
        wh = w[:, h * QK_HEAD:(h + 1) * QK_HEAD]
        pe = wh[:, QK_NOPE:]
        cols += [wh[:, :QK_NOPE], pe, _rot_half_cols(pe)]
    return jnp.concatenate(cols, axis=1).astype(BF16)


def _tile(n, pref):
    return pref if n % pref == 0 else n


def kernel(x, positions, w_in, conv_w, q_norm_g, w_uq, kv_norm_g, w_ukv, rwkv_mu, rwkv_w0,
           rwkv_w2, rwkv_a0, rwkv_a2, rwkv_k_k, rwkv_k_a, rwkv_r_k, rwkv_gn_g, rwkv_gn_b,
           w_out, ln_g, ln_b):
    batch, seq, _ = x.shape
    t = batch * seq
    row = lambda a: a.reshape(1, -1)

    inv_freq = ROPE_THETA ** (-jnp.arange(0, QK_ROPE, 2, dtype=F32) / QK_ROPE)
    invf4 = jnp.tile(inv_freq, 4).reshape(1, LANES)
    cs = _rope_table(positions.reshape(t, 1), invf4, _tile(t, 1024))

    head_of = jnp.arange(RWKV_DIM) // RWKV_N
    eones = (head_of[:, None] == head_of[None, :]).astype(BF16)
    zpad = jnp.zeros((DECAY_LORA, RWKV_DIM), F32)

    x2 = x.reshape(t, D_MODEL)
    for l in range(DEPTH):
        u = _inproj(x2, _pack_w_in(w_in[l]), _tile(t, 512), 768)
        y_conv = _conv_branch(u, conv_w[l], seq, _tile(seq, 512))
        q, k, v = _mla_prep(u, cs, row(q_norm_g[l]), _pack_w_uq(w_uq[l]), row(kv_norm_g[l]),
                            w_ukv[l].astype(BF16), batch, seq, _tile(seq, 512))
        y_mla = _attention(q, k, v, u, seq, _tile(seq, 1024), 256)
        mu_r, mu_k, mu_v, mu_wa = _pack_mu(rwkv_mu[l])
        w2p = jnp.concatenate([rwkv_w2[l], zpad], axis=0).astype(BF16)
        a2p = jnp.concatenate([zpad, rwkv_a2[l]], axis=0).astype(BF16)
        rr, lw, kk, vv, aa, bb = _rwkv_prep(
            u, mu_r, mu_k, mu_v, mu_wa, row(rwkv_w0[l]), w2p, row(rwkv_a0[l]), a2p,
            row(rwkv_k_k[l]), row(rwkv_k_a[l]), eones, seq, _tile(seq, 512))
        y_rwkv = _rwkv_scan(rr, lw, kk, vv, aa, bb, u, row(rwkv_gn_g[l]), row(rwkv_gn_b[l]),
                            row(rwkv_r_k[l]), eones, batch, seq, _tile(seq, 512))
        wo = w_out[l].astype(BF16)
        x2 = _outproj(y_conv, y_mla, y_rwkv, x2,
                      wo[:CONV_DIM], wo[CONV_DIM:CONV_DIM + MLA_DIM], wo[CONV_DIM + MLA_DIM:],
                      row(ln_g[l]), row(ln_b[l]), _tile(t, 256))
    return x2.reshape(batch, seq, D_MODEL)
```

```python
import functools

import jax
import jax.numpy as jnp
from jax import lax
from jax.experimental import pallas as pl
from jax.experimental.pallas import tpu as pltpu

F32 = jnp.float32
BF16 = jnp.bfloat16

D_MODEL = 2048
CONV_DIM = 512
CONV_K = 3
MLA_HEADS = 8
QK_NOPE = 128
QK_ROPE = 64
QK_HEAD = QK_NOPE + QK_ROPE
V_DIM = 128
MLA_DIM = MLA_HEADS * V_DIM
Q_LORA = 512
KV_LORA = 256
ROPE_THETA = 10000.0
RWKV_HEADS = 8
RWKV_N = 64
RWKV_DIM = RWKV_HEADS * RWKV_N
DECAY_LORA = 64
A_LORA = 64
RWKV_GN_EPS = 64e-5
LN_EPS = 1e-5
RMS_EPS = 1e-6
DEPTH = 2
DEEPNORM_ALPHA = (2 * DEPTH) ** 0.25

U_CB, U_CC, U_CH, U_CG = 0, 512, 1024, 1536
U_CQ = 2048
U_R, U_K, U_V, U_RG = 2560, 3072, 3584, 4096
U_MG = 4608
U_CKV = 5632
U_KPE = 5888
U_WA = 6016
U_TOTAL = 6144

LANES = 128
PREV_ROWS = 16
CHUNK = 64
VMEM_LIMIT = 56 * 1024 * 1024

LOG2E = 1.4426950408889634


def _cparams(sem):
    return pltpu.CompilerParams(dimension_semantics=sem, vmem_limit_bytes=VMEM_LIMIT)


def _bdot(a, b):
    return jnp.dot(a.astype(BF16), b.astype(BF16), preferred_element_type=F32)


def _inproj_kernel(x_ref, w_ref, o_ref, xb_ref):
    @pl.when(pl.program_id(1) == 0)
    def _():
        xb_ref[...] = x_ref[...].astype(BF16)

    o_ref[...] = jnp.dot(xb_ref[...], w_ref[...], preferred_element_type=F32).astype(o_ref.dtype)


def _inproj(x2, w_p, tm, tn):
    t = x2.shape[0]
    return pl.pallas_call(
        _inproj_kernel,
        grid=(t // tm, U_TOTAL // tn),
        in_specs=[pl.BlockSpec((tm, D_MODEL), lambda i, j: (i, 0)),
                  pl.BlockSpec((D_MODEL, tn), lambda i, j: (0, j))],
        out_specs=pl.BlockSpec((tm, tn), lambda i, j: (i, j)),
        out_shape=jax.ShapeDtypeStruct((t, U_TOTAL), BF16),
        scratch_shapes=[pltpu.VMEM((tm, D_MODEL), BF16)],
        compiler_params=_cparams(("parallel", "arbitrary")),
        name="inproj",
    )(x2, w_p)


def _shift_rows(x, prev, n, row):
    out = pltpu.roll(x, n, 0)
    for r in range(n):
        out = jnp.where(row == r, prev[PREV_ROWS - n + r:PREV_ROWS - n + r + 1], out)
    return out


def _conv_kernel(cb, cc, ch, cg, ccp, chp, w, o, *, tiles_per_seq):
    first = (pl.program_id(0) % tiles_per_seq) == 0
    up = cc[...].astype(F32) * ch[...].astype(F32)
    prev = jnp.where(first, 0.0, ccp[...].astype(F32) * chp[...].astype(F32))
    row = lax.broadcasted_iota(jnp.int32, up.shape, 0)
    u1 = _shift_rows(up, prev, 1, row)
    u2 = _shift_rows(up, prev, 2, row)
    wv = w[...]
    y = wv[0:1] * u2 + wv[1:2] * u1 + wv[2:3] * up
    g = cg[...].astype(F32)
    o[...] = (cb[...].astype(F32) * y * (g * jax.nn.sigmoid(g))).astype(o.dtype)


def _conv_branch(u, conv_w, seq, tm):
    t = u.shape[0]
    cblk = lambda c: pl.BlockSpec((tm, CONV_DIM), lambda i, c=c: (i, c))
    pblk = lambda c: pl.BlockSpec((PREV_ROWS, CONV_DIM),
                                  lambda i, c=c: (jnp.maximum(i * (tm // PREV_ROWS) - 1, 0), c))
    return pl.pallas_call(
        functools.partial(_conv_kernel, tiles_per_seq=seq // tm),
        grid=(t // tm,),
        in_specs=[cblk(U_CB // 512), cblk(U_CC // 512), cblk(U_CH // 512), cblk(U_CG // 512),
                  pblk(U_CC // 512), pblk(U_CH // 512),
                  pl.BlockSpec((CONV_K, CONV_DIM), lambda i: (0, 0))],
        out_specs=pl.BlockSpec((tm, CONV_DIM), lambda i: (i, 0)),
        out_shape=jax.ShapeDtypeStruct((t, CONV_DIM), BF16),
        compiler_params=_cparams(("parallel",)),
        name="conv_branch",
    )(u, u, u, u, u, u, conv_w)


def _rope_kernel(pos_ref, invf_ref, cs_ref):
    ang = pos_ref[...].astype(F32) * invf_ref[...]
    lane = lax.broadcasted_iota(jnp.int32, ang.shape, 1)
    cs_ref[...] = jnp.where(lane < QK_ROPE, jnp.cos(ang), jnp.sin(ang))


def _rope_table(pos2, invf4, tm):
    t = pos2.shape[0]
    return pl.pallas_call(
        _rope_kernel,
        grid=(t // tm,),
        in_specs=[pl.BlockSpec((tm, 1), lambda i: (i, 0)),
                  pl.BlockSpec((1, LANES), lambda i: (0, 0))],
        out_specs=pl.BlockSpec((tm, LANES), lambda i: (i, 0)),
        out_shape=jax.ShapeDtypeStruct((t, LANES), F32),
        compiler_params=_cparams(("parallel",)),
        name="rope_table",
    )(pos2, invf4)


def _rms(x, g):
    return x * lax.rsqrt(jnp.mean(x * x, -1, keepdims=True) + RMS_EPS) * g


def _mla_prep_kernel(cq_ref, ckv_ref, kp_ref, cs_ref, qg_ref, wq_ref, kg_ref, wkv_ref,
                     q_out, k_out, v_out, *, qscale):
    cs = cs_ref[...]
    qf = _bdot(_rms(cq_ref[...].astype(F32), qg_ref[...]), wq_ref[...])
    kvf = _bdot(_rms(ckv_ref[...].astype(F32), kg_ref[...]), wkv_ref[...])
    kp = kp_ref[...].astype(F32) * cs
    kpe = (kp + pltpu.roll(kp, QK_ROPE, 1))[:, :QK_ROPE].astype(BF16)
    for h in range(MLA_HEADS):
        c0 = 2 * LANES * h
        pp = qf[:, c0 + LANES:c0 + 2 * LANES] * cs
        pe = pp + pltpu.roll(pp, QK_ROPE, 1)
        q_out[h, :, 0:QK_NOPE] = (qf[:, c0:c0 + LANES] * qscale).astype(BF16)
        q_out[h, :, QK_NOPE:QK_HEAD] = (pe[:, :QK_ROPE] * qscale).astype(BF16)
        k_out[h, :, 0:QK_NOPE] = kvf[:, c0:c0 + LANES].astype(BF16)
        k_out[h, :, QK_NOPE:QK_HEAD] = kpe
        v_out[h, :, 0:V_DIM] = kvf[:, c0 + LANES:c0 + 2 * LANES].astype(BF16)
        v_out[h, :, V_DIM:2 * V_DIM] = jnp.ones((kvf.shape[0], V_DIM), BF16)


def _mla_prep(u, cs, qg, wq_p, kg, wkv, batch, seq, tm):
    t = u.shape[0]
    nt = seq // tm
    hblk = lambda d: pl.BlockSpec((None, MLA_HEADS, tm, d), lambda i: (i // nt, 0, i % nt, 0))
    full = lambda a: pl.BlockSpec(a.shape, lambda i: (0,) * a.ndim)
    qscale = QK_HEAD ** -0.5 * LOG2E
    return pl.pallas_call(
        functools.partial(_mla_prep_kernel, qscale=qscale),
        grid=(t // tm,),
        in_specs=[pl.BlockSpec((tm, Q_LORA), lambda i: (i, U_CQ // Q_LORA)),
                  pl.BlockSpec((tm, KV_LORA), lambda i: (i, U_CKV // KV_LORA)),
                  pl.BlockSpec((tm, LANES), lambda i: (i, U_KPE // LANES)),
                  pl.BlockSpec((tm, LANES), lambda i: (i, 0)),
                  full(qg), full(wq_p), full(kg), full(wkv)],
        out_specs=[hblk(QK_HEAD), hblk(QK_HEAD), hblk(2 * V_DIM)],
        out_shape=[jax.ShapeDtypeStruct((batch, MLA_HEADS, seq, QK_HEAD), BF16),
                   jax.ShapeDtypeStruct((batch, MLA_HEADS, seq, QK_HEAD), BF16),
                   jax.ShapeDtypeStruct((batch, MLA_HEADS, seq, 2 * V_DIM), BF16)],
        compiler_params=_cparams(("parallel",)),
        name="mla_prep",
    )(u, u, u, cs, qg, wq_p, kg, wkv)


def _attn_kernel(q_ref, k_ref, v_ref, g_ref, o_ref, m_ref, acc_ref, s0, s1, p0, p1, a0, a1, x0, x1, *, tq, sub):
    tk = tq // 2
    i = pl.program_id(2)
    s_scr, p_scr, a_scr, x_scr = (s0, s1), (p0, p1), (a0, a1), (x0, x1)
    lo, hi, full = slice(0, tk), slice(tk, tq), slice(0, tq)
    nt = (((1,), (1,)), ((), ()))

    m_ref[...] = jnp.full(m_ref.shape, -jnp.inf, F32)
    acc_ref[...] = jnp.zeros(acc_ref.shape, F32)
    p1[...] = jnp.zeros(p1.shape, BF16)
    a1[...] = jnp.ones(a1.shape, F32)

    def qk(t, slot, rows):
        kb = k_ref[pl.ds(pl.multiple_of(t * tk, tk), tk), :]
        for r0 in range(rows.start, rows.stop, sub):
            rs = slice(r0, r0 + sub)
            s = lax.dot_general(q_ref[rs, :], kb, nt, preferred_element_type=F32)
            s_scr[slot][rs, :] = s
            x_scr[slot][rs, :] = jnp.broadcast_to(jnp.max(s, axis=1, keepdims=True), (sub, LANES))

    def softmax(slot, rows, causal):
        for r0 in range(rows.start, rows.stop, sub):
            rs = slice(r0, r0 + sub)
            s = s_scr[slot][rs, :]
            if causal:
                r = lax.broadcasted_iota(jnp.int32, s.shape, 0) + (r0 - rows.start)
                c = lax.broadcasted_iota(jnp.int32, s.shape, 1)
                s = jnp.where(c <= r, s, -jnp.inf)
                s_max = jnp.max(s, axis=1, keepdims=True)
            else:
                s_max = x_scr[slot][rs, :]
            m_prev = m_ref[rs, :]
            m_new = jnp.maximum(m_prev, s_max)
            p_scr[slot][rs, :] = jnp.exp2(s - pltpu.repeat(m_new, tk // LANES, 1)).astype(BF16)
            a_scr[slot][rs, :] = jnp.exp2(m_prev - m_new)
            m_ref[rs, :] = m_new

    def pv(t, slot, rows):
        vb = v_ref[pl.ds(pl.multiple_of(t * tk, tk), tk), :]
        acc_ref[rows, :] = (pltpu.repeat(a_scr[slot][rows, :], 2, 1) * acc_ref[rows, :]
                            + jnp.dot(p_scr[slot][rows, :], vb, preferred_element_type=F32))

    def step(t, slot):
        qk(t + 1, 1 - slot, full)
        pv(jnp.maximum(t - 1, 0), 1 - slot, full)
        softmax(slot, full, False)

    def pair(tt, carry):
        step(2 * tt, 0)
        step(2 * tt + 1, 1)
        return carry

    qk(0, 0, full)
    lax.fori_loop(0, i, pair, 0)
    d = 2 * i
    qk(d + 1, 1, hi)
    pv(jnp.maximum(d - 1, 0), 1, full)
    softmax(0, lo, True)
    softmax(0, hi, False)
    softmax(1, hi, True)
    pv(d, 0, full)
    pv(d + 1, 1, hi)
    g = g_ref[...].astype(F32)
    acc = acc_ref[...]
    o_ref[...] = (acc[:, :V_DIM] / acc[:, V_DIM:] * (g * jax.nn.sigmoid(g))).astype(o_ref.dtype)


def _attention(q, k, v_aug, u, seq, tq, sub):
    batch = q.shape[0]
    t = batch * seq
    nq = seq // tq
    return pl.pallas_call(
        functools.partial(_attn_kernel, tq=tq, sub=sub),
        grid=(batch, MLA_HEADS, nq),
        in_specs=[pl.BlockSpec((None, None, tq, QK_HEAD), lambda b, h, i: (b, h, i, 0)),
                  pl.BlockSpec((None, None, seq, QK_HEAD), lambda b, h, i: (b, h, 0, 0)),
                  pl.BlockSpec((None, None, seq, 2 * V_DIM), lambda b, h, i: (b, h, 0, 0)),
                  pl.BlockSpec((tq, V_DIM), lambda b, h, i: (b * nq + i, U_MG // V_DIM + h))],
        out_specs=pl.BlockSpec((tq, V_DIM), lambda b, h, i: (b * nq + i, h)),
        out_shape=jax.ShapeDtypeStruct((t, MLA_DIM), BF16),
        scratch_shapes=[pltpu.VMEM((tq, LANES), F32), pltpu.VMEM((tq, 2 * V_DIM), F32),
                        pltpu.VMEM((tq, tq // 2), F32), pltpu.VMEM((tq, tq // 2), F32),
                        pltpu.VMEM((tq, tq // 2), BF16), pltpu.VMEM((tq, tq // 2), BF16),
                        pltpu.VMEM((tq, LANES), F32), pltpu.VMEM((tq, LANES), F32),
                        pltpu.VMEM((tq, LANES), F32), pltpu.VMEM((tq, LANES), F32)],
        compiler_params=_cparams(("parallel", "parallel", "arbitrary")),
        name="mla_attention",
    )(q, k, v_aug, u)


def _softplus(z):
    return jnp.maximum(z, 0.0) + jnp.log1p(jnp.exp(-jnp.abs(z)))


def _rwkv_prep_kernel(r_ref, k_ref, v_ref, wa_ref, rp_ref, kp_ref, vp_ref, wap_ref,
                      mu_r, mu_k, mu_v, mu_wa, w0, w2p, a0, a2p, kkw, kaw, eones,
                      r_o, lw_o, k_o, v_o, a_o, b_o, *, tiles_per_seq):
    first = (pl.program_id(0) % tiles_per_seq) == 0
    row = lax.broadcasted_iota(jnp.int32, r_ref.shape, 0)
    row_wa = lax.broadcasted_iota(jnp.int32, wa_ref.shape, 0)

    def lerp(x_ref, p_ref, mu, rw):
        x = x_ref[...].astype(F32)
        prev = jnp.where(first, 0.0, p_ref[...].astype(F32))
        return x + (_shift_rows(x, prev, 1, rw) - x) * mu[...]

    r = lerp(r_ref, rp_ref, mu_r, row)
    k = lerp(k_ref, kp_ref, mu_k, row)
    v = lerp(v_ref, vp_ref, mu_v, row)
    wa = lerp(wa_ref, wap_ref, mu_wa, row_wa)
    w = -_softplus(-(w0[...] + _bdot(jnp.tanh(wa), w2p[...]))) - 0.5
    a_sig = jax.nn.sigmoid(a0[...] + _bdot(wa, a2p[...]))
    kk = k * kkw[...]
    ss = _bdot(kk * kk, eones[...])
    kk = kk * (1.0 / jnp.maximum(jnp.sqrt(ss), 1e-12))
    r_o[...] = r
    lw_o[...] = -jnp.exp(w)
    k_o[...] = k * (1.0 + (a_sig - 1.0) * kaw[...])
    v_o[...] = v
    a_o[...] = -kk
    b_o[...] = kk * a_sig


def _rwkv_prep(u, mu_r, mu_k, mu_v, mu_wa, w0, w2p, a0, a2p, kkw, kaw, eones, seq, tm):
    t = u.shape[0]
    cblk = lambda c: pl.BlockSpec((tm, RWKV_DIM), lambda i, c=c: (i, c))
    prow = lambda i: jnp.maximum(i * (tm // PREV_ROWS) - 1, 0)
    pblk = lambda c: pl.BlockSpec((PREV_ROWS, RWKV_DIM), lambda i, c=c: (prow(i), c))
    full = lambda a: pl.BlockSpec(a.shape, lambda i: (0,) * a.ndim)
    out = jax.ShapeDtypeStruct((t, RWKV_DIM), F32)
    oblk = pl.BlockSpec((tm, RWKV_DIM), lambda i: (i, 0))
    return pl.pallas_call(
        functools.partial(_rwkv_prep_kernel, tiles_per_seq=seq // tm),
        grid=(t // tm,),
        in_specs=[cblk(U_R // 512), cblk(U_K // 512), cblk(U_V // 512),
                  pl.BlockSpec((tm, LANES), lambda i: (i, U_WA // LANES)),
                  pblk(U_R // 512), pblk(U_K // 512), pblk(U_V // 512),
                  pl.BlockSpec((PREV_ROWS, LANES), lambda i: (prow(i), U_WA // LANES)),
                  full(mu_r), full(mu_k), full(mu_v), full(mu_wa), full(w0), full(w2p),
                  full(a0), full(a2p), full(kkw), full(kaw), full(eones)],
        out_specs=[oblk] * 6,
        out_shape=[out] * 6,
        compiler_params=_cparams(("parallel",)),
        name="rwkv_prep",
    )(u, u, u, u, u, u, u, u, mu_r, mu_k, mu_v, mu_wa, w0, w2p, a0, a2p, kkw, kaw, eones)


def _stack(x, even_lane):
    return jnp.concatenate([jnp.where(even_lane, x, 0.0), jnp.where(even_lane, 0.0, x)], axis=0)


def _unstack(x):
    return x[:CHUNK] + x[CHUNK:]


def _each(fn, *lists):
    return [fn(*args) for args in zip(*lists)]


def _chunk_terms(r, lw, k, v, a, b, masks):
    even_lane, strict, incl, same_blk, eye, tri = masks
    nt = (((1,), (1,)), ((), ()))
    tn = (((0,), (0,)), ((), ()))

    def cumsum(x):
        hi = x.astype(BF16)
        lo = (x - hi.astype(F32)).astype(BF16)
        return (jnp.dot(tri, hi, preferred_element_type=F32)
                + jnp.dot(tri, lo, preferred_element_type=F32))

    cum = _each(cumsum, lw)
    cum_end = _each(lambda c: c[CHUNK - 1:CHUNK], cum)
    at_s = _each(lambda x, c, w: _stack(x * jnp.exp(c - w), even_lane), a, cum, lw)
    rt_s = _each(lambda x, c: _stack(x * jnp.exp(c), even_lane), r, cum)
    v_s = _each(lambda x: _stack(x, even_lane), v)
    bt = _each(lambda x, c: (x * jnp.exp(-c)).astype(BF16), b, cum)
    kt = _each(lambda x, c: (x * jnp.exp(-c)).astype(BF16), k, cum)
    g = _each(lambda x, y, p, q: lax.dot_general(
        jnp.concatenate([x, y], axis=0).astype(BF16), jnp.concatenate([p, p, q, q], axis=0), nt,
        preferred_element_type=F32), at_s, rt_s, bt, kt)
    x = _each(lambda m: jnp.where(strict, m[:LANES, :LANES], 0.0), g)
    a_ak = _each(lambda m: jnp.where(strict, m[:LANES, LANES:], 0.0), g)
    a_r = _each(lambda m: jnp.concatenate([jnp.where(incl, m[LANES:, :LANES], 0.0),
                                           jnp.where(incl, m[LANES:, LANES:], 0.0)], axis=1), g)
    akv = _each(_bdot, a_ak, v_s)

    tinv = _each(lambda m: jnp.where(eye, 1.0, 0.0) + m, x)
    n = 2
    while n < CHUNK:
        x = _each(lambda m: _bdot(m, m), x)
        tinv = _each(lambda t, m: t + _bdot(t, m), tinv, x)
        n *= 2

    wu = _each(lambda t, p, q: _bdot(t, jnp.concatenate([p, q], axis=1)), tinv, at_s, akv)
    zeros = jnp.zeros((LANES, LANES), F32)
    top = _each(lambda m, w, q: _bdot(m, jnp.concatenate(
        [w, jnp.concatenate([zeros, q], axis=1)], axis=0)), a_r, wu, v_s)
    rw_s = _each(lambda x, t: x + t[:, :LANES], rt_s, top)
    y0_s = _each(lambda t: t[:, LANES:], top)

    def carry_terms(bb, kk, vv, c, ce, w):
        to_end = jnp.exp(ce - c)
        bk = jnp.concatenate([bb * to_end, kk * to_end], axis=0)
        rhs = jnp.concatenate(
            [jnp.concatenate([_unstack(w[:, :LANES]), _unstack(w[:, LANES:])], axis=1),
             jnp.concatenate([jnp.zeros((CHUNK, LANES), F32), vv], axis=1)], axis=0)
        mz = lax.dot_general(bk.astype(BF16), rhs.astype(BF16), tn, preferred_element_type=F32)
        m = jnp.where(same_blk, mz[:, :LANES], 0.0) + jnp.where(eye, jnp.exp(ce), 0.0)
        return m, jnp.where(same_blk, mz[:, LANES:], 0.0)

    mz = _each(carry_terms, b, k, v, cum, cum_end, wu)
    return rw_s, y0_s, [t[0] for t in mz], [t[1] for t in mz]


def _rwkv_scan_kernel(r_ref, lw_ref, k_ref, v_ref, a_ref, b_ref, g_ref, gng, gnb, rkw, eones,
                      o_ref, state_ref, y_ref, *, tt):
    @pl.when(pl.program_id(0) == 0)
    def _():
        state_ref[...] = jnp.zeros(state_ref.shape, F32)

    rr = lax.broadcasted_iota(jnp.int32, (LANES, LANES), 0)
    cc = lax.broadcasted_iota(jnp.int32, (LANES, LANES), 1)
    same_blk = (rr // CHUNK) == (cc // CHUNK)
    strict = same_blk & ((cc % CHUNK) < (rr % CHUNK))
    incl = same_blk & ((cc % CHUNK) <= (rr % CHUNK))
    eye = rr == cc
    even_lane = lax.broadcasted_iota(jnp.int32, (CHUNK, LANES), 1) < CHUNK
    tr = lax.broadcasted_iota(jnp.int32, (CHUNK, CHUNK), 0)
    tc = lax.broadcasted_iota(jnp.int32, (CHUNK, CHUNK), 1)
    tri = jnp.where(tc <= tr, 1.0, 0.0).astype(BF16)
    masks = (even_lane, strict, incl, same_blk, eye, tri)

    batch = r_ref.shape[0]
    chains = [(bi, slice(p * LANES, (p + 1) * LANES))
              for bi in range(batch) for p in range(RWKV_DIM // LANES)]

    def chunk_body(c, carry):
        rows = pl.ds(pl.multiple_of(c * CHUNK, CHUNK), CHUNK)
        load = lambda ref: [ref[bi, rows, cs] for bi, cs in chains]
        rw_s, y0_s, m, z0 = _chunk_terms(load(r_ref), load(lw_ref), load(k_ref), load(v_ref),
                                         load(a_ref), load(b_ref), masks)
        ys = [_bdot(jnp.concatenate([rw_s[n], m[n]], axis=0), state_ref[n])
              for n in range(len(chains))]
        for n, (bi, cs) in enumerate(chains):
            y_ref[bi, rows, cs] = _unstack(ys[n][:LANES] + y0_s[n])
            state_ref[n] = ys[n][LANES:] + z0[n]
        return carry

    lax.fori_loop(0, tt // CHUNK, chunk_body, 0)

    e = eones[...]

    def headsum(x):
        hi = x.astype(BF16)
        lo = (x - hi.astype(F32)).astype(BF16)
        return (jnp.dot(hi, e, preferred_element_type=F32)
                + jnp.dot(lo, e, preferred_element_type=F32))

    for bi in range(batch):
        y = y_ref[bi]
        d = y - headsum(y) * (1.0 / RWKV_N)
        var = headsum(d * d) * (1.0 / RWKV_N)
        yn = d * lax.rsqrt(var + RWKV_GN_EPS) * gng[...] + gnb[...]
        yn = yn + headsum(r_ref[bi] * k_ref[bi] * rkw[...]) * v_ref[bi]
        g = g_ref[bi].astype(F32)
        o_ref[bi] = (yn * (g * jax.nn.sigmoid(g))).astype(o_ref.dtype)


def _rwkv_scan(r, lw, k, v, a, b, u, gng, gnb, rkw, eones, batch, seq, tt):
    to3 = lambda x: x.reshape(batch, seq, x.shape[-1])
    blk = pl.BlockSpec((batch, tt, RWKV_DIM), lambda i: (0, i, 0))
    full = lambda x: pl.BlockSpec(x.shape, lambda i: (0,) * x.ndim)
    out = pl.pallas_call(
        functools.partial(_rwkv_scan_kernel, tt=tt),
        grid=(seq // tt,),
        in_specs=[blk] * 6 + [pl.BlockSpec((batch, tt, RWKV_DIM), lambda i: (0, i, U_RG // RWKV_DIM)),
                              full(gng), full(gnb), full(rkw), full(eones)],
        out_specs=blk,
        out_shape=jax.ShapeDtypeStruct((batch, seq, RWKV_DIM), BF16),
        scratch_shapes=[pltpu.VMEM((batch * RWKV_DIM // LANES, LANES, LANES), F32),
                        pltpu.VMEM((batch, tt, RWKV_DIM), F32)],
        compiler_params=_cparams(("arbitrary",)),
        name="rwkv_scan",
    )(to3(r), to3(lw), to3(k), to3(v), to3(a), to3(b), to3(u), gng, gnb, rkw, eones)
    return out.reshape(batch * seq, RWKV_DIM)


def _outproj_kernel(yc, ym, yr, x_ref, w1, w2, w3, lg, lb, o_ref):
    acc = jnp.dot(yc[...], w1[...], preferred_element_type=F32)
    acc = acc + jnp.dot(ym[...], w2[...], preferred_element_type=F32)
    acc = acc + jnp.dot(yr[...], w3[...], preferred_element_type=F32)
    z = DEEPNORM_ALPHA * x_ref[...] + acc
    mu = jnp.mean(z, -1, keepdims=True)
    d = z - mu
    var = jnp.mean(d * d, -1, keepdims=True)
    o_ref[...] = d * lax.rsqrt(var + LN_EPS) * lg[...] + lb[...]


def _outproj(yc, ym, yr, x2, w1, w2, w3, lg, lb, tm):
    t = x2.shape[0]
    rblk = lambda d: pl.BlockSpec((tm, d), lambda i: (i, 0))
    full = lambda a: pl.BlockSpec(a.shape, lambda i: (0,) * a.ndim)
    return pl.pallas_call(
        _outproj_kernel,
        grid=(t // tm,),
        in_specs=[rblk(CONV_DIM), rblk(MLA_DIM), rblk(RWKV_DIM), rblk(D_MODEL),
                  full(w1), full(w2), full(w3), full(lg), full(lb)],
        out_specs=rblk(D_MODEL),
        out_shape=jax.ShapeDtypeStruct((t, D_MODEL), F32),
        compiler_params=_cparams(("parallel",)),
        name="outproj_ln",
    )(yc, ym, yr, x2, w1, w2, w3, lg, lb)


def _rot_half_cols(w):
    return jnp.concatenate([-w[:, QK_ROPE // 2:], w[:, :QK_ROPE // 2]], axis=1)


def _pack_w_in(w):
    o_ckv, o_kpe, o_mg, o_rc = 2560, 2816, 2880, 3904
    ckv = w[:, o_ckv:o_kpe]
    kpe = w[:, o_kpe:o_mg]
    mg = w[:, o_mg:o_rc]
    r = w[:, o_rc:o_rc + 512]
    wd = w[:, o_rc + 512:o_rc + 576]
    k = w[:, o_rc + 576:o_rc + 1088]
    v = w[:, o_rc + 1088:o_rc + 1600]
    ad = w[:, o_rc + 1600:o_rc + 1664]
    rg = w[:, o_rc + 1664:]
    return jnp.concatenate([w[:, :o_ckv], r, k, v, rg, mg, ckv, kpe, _rot_half_cols(kpe),
                            wd, ad], axis=1).astype(BF16)


def _pack_mu(mu):
    r, wd, k, v, ad = (mu[0:512], mu[512:576], mu[576:1088], mu[1088:1600], mu[1600:1664])
    row = lambda a: a.reshape(1, -1)
    return row(r), row(k), row(v), row(jnp.concatenate([wd, ad]))


def _pack_w_uq(w):
    cols = []
    for h in range(MLA_HEADS):
        wh = w[:, h * QK_HEAD:(h + 1) * QK_HEAD]
        pe = wh[:, QK_NOPE:]
        cols += [wh[:, :QK_NOPE], pe, _rot_half_cols(pe)]
    return jnp.concatenate(cols, axis=1).astype(BF16)


def _tile(n, pref):
    return pref if n % pref == 0 else n


def kernel(x, positions, w_in, conv_w, q_norm_g, w_uq, kv_norm_g, w_ukv, rwkv_mu, rwkv_w0,
           rwkv_w2, rwkv_a0, rwkv_a2, rwkv_k_k, rwkv_k_a, rwkv_r_k, rwkv_gn_g, rwkv_gn_b,
           w_out, ln_g, ln_b):
    batch, seq, _ = x.shape
    t = batch * seq
    row = lambda a: a.reshape(1, -1)

    inv_freq = ROPE_THETA ** (-jnp.arange(0, QK_ROPE, 2, dtype=F32) / QK_ROPE)
    invf4 = jnp.tile(inv_freq, 4).reshape(1, LANES)
    cs = _rope_table(positions.reshape(t, 1), invf4, _tile(t, 1024))

    head_of = jnp.arange(RWKV_DIM) // RWKV_N
    eones = (head_of[:, None] == head_of[None, :]).astype(BF16)
    zpad = jnp.zeros((DECAY_LORA, RWKV_DIM), F32)

    x2 = x.reshape(t, D_MODEL)
    for l in range(DEPTH):
        u = _inproj(x2, _pack_w_in(w_in[l]), _tile(t, 1024), 768)
        y_conv = _conv_branch(u, conv_w[l], seq, _tile(seq, 512))
        q, k, v = _mla_prep(u, cs, row(q_norm_g[l]), _pack_w_uq(w_uq[l]), row(kv_norm_g[l]),
                            w_ukv[l].astype(BF16), batch, seq, _tile(seq, 512))
        y_mla = _attention(q, k, v, u, seq, _tile(seq, 1024), 256)
        mu_r, mu_k, mu_v, mu_wa = _pack_mu(rwkv_mu[l])
        w2p = jnp.concatenate([rwkv_w2[l], zpad], axis=0).astype(BF16)
        a2p = jnp.concatenate([zpad, rwkv_a2[l]], axis=0).astype(BF16)
        rr, lw, kk, vv, aa, bb = _rwkv_prep(
            u, mu_r, mu_k, mu_v, mu_wa, row(rwkv_w0[l]), w2p, row(rwkv_a0[l]), a2p,
            row(rwkv_k_k[l]), row(rwkv_k_a[l]), eones, seq, _tile(seq, 512))
        y_rwkv = _rwkv_scan(rr, lw, kk, vv, aa, bb, u, row(rwkv_gn_g[l]), row(rwkv_gn_b[l]),
                            row(rwkv_r_k[l]), eones, batch, seq, _tile(seq, 512))
        wo = w_out[l].astype(BF16)
        x2 = _outproj(y_conv, y_mla, y_rwkv, x2,
                      wo[:CONV_DIM], wo[CONV_DIM:CONV_DIM + MLA_DIM], wo[CONV_DIM + MLA_DIM:],
                      row(ln_g[l]), row(ln_b[l]), _tile(t, 256))
    return x2.reshape(batch, seq, D_MODEL)
```

```python
import functools

import jax
import jax.numpy as jnp
from jax import lax
from jax.experimental import pallas as pl
from jax.experimental.pallas import tpu as pltpu

F32 = jnp.float32
BF16 = jnp.bfloat16

D_MODEL = 2048
CONV_DIM = 512
CONV_K = 3
MLA_HEADS = 8
QK_NOPE = 128
QK_ROPE = 64
QK_HEAD = QK_NOPE + QK_ROPE
V_DIM = 128
MLA_DIM = MLA_HEADS * V_DIM
Q_LORA = 512
KV_LORA = 256
ROPE_THETA = 10000.0
RWKV_HEADS = 8
RWKV_N = 64
RWKV_DIM = RWKV_HEADS * RWKV_N
DECAY_LORA = 64
A_LORA = 64
RWKV_GN_EPS = 64e-5
LN_EPS = 1e-5
RMS_EPS = 1e-6
DEPTH = 2
DEEPNORM_ALPHA = (2 * DEPTH) ** 0.25

U_CB, U_CC, U_CH, U_CG = 0, 512, 1024, 1536
U_CQ = 2048
U_R, U_K, U_V, U_RG = 2560, 3072, 3584, 4096
U_MG = 4608
U_CKV = 5632
U_KPE = 5888
U_WA = 6016
U_TOTAL = 6144

LANES = 128
PREV_ROWS = 16
CHUNK = 64
VMEM_LIMIT = 56 * 1024 * 1024

LOG2E = 1.4426950408889634


def _cparams(sem):
    return pltpu.CompilerParams(dimension_semantics=sem, vmem_limit_bytes=VMEM_LIMIT)


def _bdot(a, b):
    return jnp.dot(a.astype(BF16), b.astype(BF16), preferred_element_type=F32)


def _inproj_kernel(x_ref, w_ref, o_ref, xb_ref):
    @pl.when(pl.program_id(1) == 0)
    def _():
        xb_ref[...] = x_ref[...].astype(BF16)

    o_ref[...] = jnp.dot(xb_ref[...], w_ref[...], preferred_element_type=F32).astype(o_ref.dtype)


def _inproj(x2, w_p, tm, tn):
    t = x2.shape[0]
    return pl.pallas_call(
        _inproj_kernel,
        grid=(t // tm, U_TOTAL // tn),
        in_specs=[pl.BlockSpec((tm, D_MODEL), lambda i, j: (i, 0)),
                  pl.BlockSpec((D_MODEL, tn), lambda i, j: (0, j))],
        out_specs=pl.BlockSpec((tm, tn), lambda i, j: (i, j)),
        out_shape=jax.ShapeDtypeStruct((t, U_TOTAL), BF16),
        scratch_shapes=[pltpu.VMEM((tm, D_MODEL), BF16)],
        compiler_params=_cparams(("parallel", "arbitrary")),
        name="inproj",
    )(x2, w_p)


def _shift_rows(x, prev, n, row):
    out = pltpu.roll(x, n, 0)
    for r in range(n):
        out = jnp.where(row == r, prev[PREV_ROWS - n + r:PREV_ROWS - n + r + 1], out)
    return out


def _conv_kernel(cb, cc, ch, cg, ccp, chp, w, o, *, tiles_per_seq):
    first = (pl.program_id(0) % tiles_per_seq) == 0
    up = cc[...].astype(F32) * ch[...].astype(F32)
    prev = jnp.where(first, 0.0, ccp[...].astype(F32) * chp[...].astype(F32))
    row = lax.broadcasted_iota(jnp.int32, up.shape, 0)
    u1 = _shift_rows(up, prev, 1, row)
    u2 = _shift_rows(up, prev, 2, row)
    wv = w[...]
    y = wv[0:1] * u2 + wv[1:2] * u1 + wv[2:3] * up
    g = cg[...].astype(F32)
    o[...] = (cb[...].astype(F32) * y * (g * jax.nn.sigmoid(g))).astype(o.dtype)


def _conv_branch(u, conv_w, seq, tm):
    t = u.shape[0]
    cblk = lambda c: pl.BlockSpec((tm, CONV_DIM), lambda i, c=c: (i, c))
    pblk = lambda c: pl.BlockSpec((PREV_ROWS, CONV_DIM),
                                  lambda i, c=c: (jnp.maximum(i * (tm // PREV_ROWS) - 1, 0), c))
    return pl.pallas_call(
        functools.partial(_conv_kernel, tiles_per_seq=seq // tm),
        grid=(t // tm,),
        in_specs=[cblk(U_CB // 512), cblk(U_CC // 512), cblk(U_CH // 512), cblk(U_CG // 512),
                  pblk(U_CC // 512), pblk(U_CH // 512),
                  pl.BlockSpec((CONV_K, CONV_DIM), lambda i: (0, 0))],
        out_specs=pl.BlockSpec((tm, CONV_DIM), lambda i: (i, 0)),
        out_shape=jax.ShapeDtypeStruct((t, CONV_DIM), BF16),
        compiler_params=_cparams(("parallel",)),
        name="conv_branch",
    )(u, u, u, u, u, u, conv_w)


def _rope_kernel(pos_ref, invf_ref, cs_ref):
    ang = pos_ref[...].astype(F32) * invf_ref[...]
    lane = lax.broadcasted_iota(jnp.int32, ang.shape, 1)
    cs_ref[...] = jnp.where(lane < QK_ROPE, jnp.cos(ang), jnp.sin(ang))


def _rope_table(pos2, invf4, tm):
    t = pos2.shape[0]
    return pl.pallas_call(
        _rope_kernel,
        grid=(t // tm,),
        in_specs=[pl.BlockSpec((tm, 1), lambda i: (i, 0)),
                  pl.BlockSpec((1, LANES), lambda i: (0, 0))],
        out_specs=pl.BlockSpec((tm, LANES), lambda i: (i, 0)),
        out_shape=jax.ShapeDtypeStruct((t, LANES), F32),
        compiler_params=_cparams(("parallel",)),
        name="rope_table",
    )(pos2, invf4)


def _rms(x, g):
    return x * lax.rsqrt(jnp.mean(x * x, -1, keepdims=True) + RMS_EPS) * g


def _mla_prep_kernel(cq_ref, ckv_ref, kp_ref, cs_ref, qg_ref, wq_ref, kg_ref, wkv_ref,
                     q_out, k_out, v_out, *, qscale):
    cs = cs_ref[...]
    qf = _bdot(_rms(cq_ref[...].astype(F32), qg_ref[...]), wq_ref[...])
    kvf = _bdot(_rms(ckv_ref[...].astype(F32), kg_ref[...]), wkv_ref[...])
    kp = kp_ref[...].astype(F32) * cs
    kpe = (kp + pltpu.roll(kp, QK_ROPE, 1))[:, :QK_ROPE].astype(BF16)
    for h in range(MLA_HEADS):
        c0 = 2 * LANES * h
        pp = qf[:, c0 + LANES:c0 + 2 * LANES] * cs
        pe = pp + pltpu.roll(pp, QK_ROPE, 1)
        q_out[h, :, 0:QK_NOPE] = (qf[:, c0:c0 + LANES] * qscale).astype(BF16)
        q_out[h, :, QK_NOPE:QK_HEAD] = (pe[:, :QK_ROPE] * qscale).astype(BF16)
        k_out[h, :, 0:QK_NOPE] = kvf[:, c0:c0 + LANES].astype(BF16)
        k_out[h, :, QK_NOPE:QK_HEAD] = kpe
        v_out[h, :, 0:V_DIM] = kvf[:, c0 + LANES:c0 + 2 * LANES].astype(BF16)
        v_out[h, :, V_DIM:2 * V_DIM] = jnp.ones((kvf.shape[0], V_DIM), BF16)


def _mla_prep(u, cs, qg, wq_p, kg, wkv, batch, seq, tm):
    t = u.shape[0]
    nt = seq // tm
    hblk = lambda d: pl.BlockSpec((None, MLA_HEADS, tm, d), lambda i: (i // nt, 0, i % nt, 0))
    full = lambda a: pl.BlockSpec(a.shape, lambda i: (0,) * a.ndim)
    qscale = QK_HEAD ** -0.5 * LOG2E
    return pl.pallas_call(
        functools.partial(_mla_prep_kernel, qscale=qscale),
        grid=(t // tm,),
        in_specs=[pl.BlockSpec((tm, Q_LORA), lambda i: (i, U_CQ // Q_LORA)),
                  pl.BlockSpec((tm, KV_LORA), lambda i: (i, U_CKV // KV_LORA)),
                  pl.BlockSpec((tm, LANES), lambda i: (i, U_KPE // LANES)),
                  pl.BlockSpec((tm, LANES), lambda i: (i, 0)),
                  full(qg), full(wq_p), full(kg), full(wkv)],
        out_specs=[hblk(QK_HEAD), hblk(QK_HEAD), hblk(2 * V_DIM)],
        out_shape=[jax.ShapeDtypeStruct((batch, MLA_HEADS, seq, QK_HEAD), BF16),
                   jax.ShapeDtypeStruct((batch, MLA_HEADS, seq, QK_HEAD), BF16),
                   jax.ShapeDtypeStruct((batch, MLA_HEADS, seq, 2 * V_DIM), BF16)],
        compiler_params=_cparams(("parallel",)),
        name="mla_prep",
    )(u, u, u, cs, qg, wq_p, kg, wkv)


def _attn_kernel(q_ref, k_ref, v_ref, g_ref, o_ref, m_ref, acc_ref, s0, s1, p0, p1, a0, a1, x0, x1, *, tq, sub):
    tk = tq // 2
    i = pl.program_id(2)
    s_scr, p_scr, a_scr, x_scr = (s0, s1), (p0, p1), (a0, a1), (x0, x1)
    lo, hi, full = slice(0, tk), slice(tk, tq), slice(0, tq)
    nt = (((1,), (1,)), ((), ()))

    m_ref[...] = jnp.full(m_ref.shape, -jnp.inf, F32)
    acc_ref[...] = jnp.zeros(acc_ref.shape, F32)
    p1[...] = jnp.zeros(p1.shape, BF16)
    a1[...] = jnp.ones(a1.shape, F32)

    def qk(t, slot, rows):
        kb = k_ref[pl.ds(pl.multiple_of(t * tk, tk), tk), :]
        for r0 in range(rows.start, rows.stop, sub):
            rs = slice(r0, r0 + sub)
            s = lax.dot_general(q_ref[rs, :], kb, nt, preferred_element_type=F32)
            s_scr[slot][rs, :] = s
            x_scr[slot][rs, :] = jnp.broadcast_to(jnp.max(s, axis=1, keepdims=True), (sub, LANES))

    def softmax(slot, rows, causal):
        for r0 in range(rows.start, rows.stop, sub):
            rs = slice(r0, r0 + sub)
            s = s_scr[slot][rs, :]
            if causal:
                r = lax.broadcasted_iota(jnp.int32, s.shape, 0) + (r0 - rows.start)
                c = lax.broadcasted_iota(jnp.int32, s.shape, 1)
                s = jnp.where(c <= r, s, -jnp.inf)
                s_max = jnp.max(s, axis=1, keepdims=True)
            else:
                s_max = x_scr[slot][rs, :]
            m_prev = m_ref[rs, :]
            m_new = jnp.maximum(m_prev, s_max)
            p_scr[slot][rs, :] = jnp.exp2(s - pltpu.repeat(m_new, tk // LANES, 1)).astype(BF16)
            a_scr[slot][rs, :] = jnp.exp2(m_prev - m_new)
            m_ref[rs, :] = m_new

    def pv(t, slot, rows):
        vb = v_ref[pl.ds(pl.multiple_of(t * tk, tk), tk), :]
        acc_ref[rows, :] = (pltpu.repeat(a_scr[slot][rows, :], 2, 1) * acc_ref[rows, :]
                            + jnp.dot(p_scr[slot][rows, :], vb, preferred_element_type=F32))

    def step(t, slot):
        qk(t + 1, 1 - slot, full)
        pv(jnp.maximum(t - 1, 0), 1 - slot, full)
        softmax(slot, full, False)

    def pair(tt):
        step(2 * tt, 0)
        step(2 * tt + 1, 1)

    qk(0, 0, full)
    odd = i % 2

    @pl.when(odd == 1)
    def _():
        pair(0)

    def two_pairs(n, carry):
        pair(odd + 2 * n)
        pair(odd + 2 * n + 1)
        return carry

    lax.fori_loop(0, i // 2, two_pairs, 0)
    d = 2 * i
    qk(d + 1, 1, hi)
    pv(jnp.maximum(d - 1, 0), 1, full)
    softmax(0, lo, True)
    softmax(0, hi, False)
    softmax(1, hi, True)
    pv(d, 0, full)
    pv(d + 1, 1, hi)
    g = g_ref[...].astype(F32)
    acc = acc_ref[...]
    o_ref[...] = (acc[:, :V_DIM] / acc[:, V_DIM:] * (g * jax.nn.sigmoid(g))).astype(o_ref.dtype)


def _attention(q, k, v_aug, u, seq, tq, sub):
    batch = q.shape[0]
    t = batch * seq
    nq = seq // tq
    return pl.pallas_call(
        functools.partial(_attn_kernel, tq=tq, sub=sub),
        grid=(batch, MLA_HEADS, nq),
        in_specs=[pl.BlockSpec((None, None, tq, QK_HEAD), lambda b, h, i: (b, h, i, 0)),
                  pl.BlockSpec((None, None, seq, QK_HEAD), lambda b, h, i: (b, h, 0, 0)),
                  pl.BlockSpec((None, None, seq, 2 * V_DIM), lambda b, h, i: (b, h, 0, 0)),
                  pl.BlockSpec((tq, V_DIM), lambda b, h, i: (b * nq + i, U_MG // V_DIM + h))],
        out_specs=pl.BlockSpec((tq, V_DIM), lambda b, h, i: (b * nq + i, h)),
        out_shape=jax.ShapeDtypeStruct((t, MLA_DIM), BF16),
        scratch_shapes=[pltpu.VMEM((tq, LANES), F32), pltpu.VMEM((tq, 2 * V_DIM), F32),
                        pltpu.VMEM((tq, tq // 2), F32), pltpu.VMEM((tq, tq // 2), F32),
                        pltpu.VMEM((tq, tq // 2), BF16), pltpu.VMEM((tq, tq // 2), BF16),
                        pltpu.VMEM((tq, LANES), F32), pltpu.VMEM((tq, LANES), F32),
                        pltpu.VMEM((tq, LANES), F32), pltpu.VMEM((tq, LANES), F32)],
        compiler_params=_cparams(("parallel", "parallel", "arbitrary")),
        name="mla_attention",
    )(q, k, v_aug, u)


DECAY_SCALE = 0.6065306597126334


def _rwkv_prep_kernel(r_ref, k_ref, v_ref, wa_ref, rp_ref, kp_ref, vp_ref, wap_ref,
                      mu_r, mu_k, mu_v, mu_wa, w0, w2p, a0, a2p, kkw, kaw, eones,
                      r_o, lw_o, k_o, v_o, a_o, b_o, *, tiles_per_seq):
    first = (pl.program_id(0) % tiles_per_seq) == 0
    row = lax.broadcasted_iota(jnp.int32, r_ref.shape, 0)
    row_wa = lax.broadcasted_iota(jnp.int32, wa_ref.shape, 0)

    def lerp(x_ref, p_ref, mu, rw):
        x = x_ref[...].astype(F32)
        prev = jnp.where(first, 0.0, p_ref[...].astype(F32))
        return x + (_shift_rows(x, prev, 1, rw) - x) * mu[...]

    r = lerp(r_ref, rp_ref, mu_r, row)
    k = lerp(k_ref, kp_ref, mu_k, row)
    v = lerp(v_ref, vp_ref, mu_v, row)
    wa = lerp(wa_ref, wap_ref, mu_wa, row_wa)
    z = w0[...] + _bdot(jnp.tanh(wa), w2p[...])
    a_sig = jax.nn.sigmoid(a0[...] + _bdot(wa, a2p[...]))
    kk = k * kkw[...]
    ss = _bdot(kk * kk, eones[...])
    kk = kk * jnp.minimum(lax.rsqrt(ss), 1e12)
    r_o[...] = r
    lw_o[...] = -DECAY_SCALE * jax.nn.sigmoid(z)
    k_o[...] = k * (1.0 + (a_sig - 1.0) * kaw[...])
    v_o[...] = v
    a_o[...] = -kk
    b_o[...] = kk * a_sig


def _rwkv_prep(u, mu_r, mu_k, mu_v, mu_wa, w0, w2p, a0, a2p, kkw, kaw, eones, seq, tm):
    t = u.shape[0]
    cblk = lambda c: pl.BlockSpec((tm, RWKV_DIM), lambda i, c=c: (i, c))
    prow = lambda i: jnp.maximum(i * (tm // PREV_ROWS) - 1, 0)
    pblk = lambda c: pl.BlockSpec((PREV_ROWS, RWKV_DIM), lambda i, c=c: (prow(i), c))
    full = lambda a: pl.BlockSpec(a.shape, lambda i: (0,) * a.ndim)
    out = jax.ShapeDtypeStruct((t, RWKV_DIM), F32)
    oblk = pl.BlockSpec((tm, RWKV_DIM), lambda i: (i, 0))
    return pl.pallas_call(
        functools.partial(_rwkv_prep_kernel, tiles_per_seq=seq // tm),
        grid=(t // tm,),
        in_specs=[cblk(U_R // 512), cblk(U_K // 512), cblk(U_V // 512),
                  pl.BlockSpec((tm, LANES), lambda i: (i, U_WA // LANES)),
                  pblk(U_R // 512), pblk(U_K // 512), pblk(U_V // 512),
                  pl.BlockSpec((PREV_ROWS, LANES), lambda i: (prow(i), U_WA // LANES)),
                  full(mu_r), full(mu_k), full(mu_v), full(mu_wa), full(w0), full(w2p),
                  full(a0), full(a2p), full(kkw), full(kaw), full(eones)],
        out_specs=[oblk] * 6,
        out_shape=[out] * 6,
        compiler_params=_cparams(("parallel",)),
        name="rwkv_prep",
    )(u, u, u, u, u, u, u, u, mu_r, mu_k, mu_v, mu_wa, w0, w2p, a0, a2p, kkw, kaw, eones)


def _stack(x, even_lane):
    return jnp.concatenate([jnp.where(even_lane, x, 0.0), jnp.where(even_lane, 0.0, x)], axis=0)


def _unstack(x):
    return x[:CHUNK] + x[CHUNK:]


def _each(fn, *lists):
    return [fn(*args) for args in zip(*lists)]


def _chunk_terms(r, lw, k, v, a, b, masks):
    even_lane, strict, incl, same_blk, eye, tri = masks
    nt = (((1,), (1,)), ((), ()))
    tn = (((0,), (0,)), ((), ()))

    def cumsum(x):
        hi = x.astype(BF16)
        lo = (x - hi.astype(F32)).astype(BF16)
        return (jnp.dot(tri, hi, preferred_element_type=F32)
                + jnp.dot(tri, lo, preferred_element_type=F32))

    cum = _each(cumsum, lw)
    cum_end = _each(lambda c: c[CHUNK - 1:CHUNK], cum)
    at_s = _each(lambda x, c, w: _stack(x * jnp.exp(c - w), even_lane), a, cum, lw)
    rt_s = _each(lambda x, c: _stack(x * jnp.exp(c), even_lane), r, cum)
    v_s = _each(lambda x: _stack(x, even_lane), v)
    bt = _each(lambda x, c: (x * jnp.exp(-c)).astype(BF16), b, cum)
    kt = _each(lambda x, c: (x * jnp.exp(-c)).astype(BF16), k, cum)
    g = _each(lambda x, y, p, q: lax.dot_general(
        jnp.concatenate([x, y], axis=0).astype(BF16), jnp.concatenate([p, p, q, q], axis=0), nt,
        preferred_element_type=F32), at_s, rt_s, bt, kt)
    x = _each(lambda m: jnp.where(strict, m[:LANES, :LANES], 0.0), g)
    a_ak = _each(lambda m: jnp.where(strict, m[:LANES, LANES:], 0.0), g)
    a_r = _each(lambda m: jnp.concatenate([jnp.where(incl, m[LANES:, :LANES], 0.0),
                                           jnp.where(incl, m[LANES:, LANES:], 0.0)], axis=1), g)
    akv = _each(_bdot, a_ak, v_s)

    tinv = _each(lambda m: jnp.where(eye, 1.0, 0.0) + m, x)
    n = 2
    while n < CHUNK:
        x = _each(lambda m: _bdot(m, m), x)
        tinv = _each(lambda t, m: t + _bdot(t, m), tinv, x)
        n *= 2

    wu = _each(lambda t, p, q: _bdot(t, jnp.concatenate([p, q], axis=1)), tinv, at_s, akv)
    zeros = jnp.zeros((LANES, LANES), F32)
    top = _each(lambda m, w, q: _bdot(m, jnp.concatenate(
        [w, jnp.concatenate([zeros, q], axis=1)], axis=0)), a_r, wu, v_s)
    rw_s = _each(lambda x, t: x + t[:, :LANES], rt_s, top)
    y0_s = _each(lambda t: t[:, LANES:], top)

    def carry_terms(bb, kk, vv, c, ce, w):
        to_end = jnp.exp(ce - c)
        bk = jnp.concatenate([bb * to_end, kk * to_end], axis=0)
        rhs = jnp.concatenate(
            [jnp.concatenate([_unstack(w[:, :LANES]), _unstack(w[:, LANES:])], axis=1),
             jnp.concatenate([jnp.zeros((CHUNK, LANES), F32), vv], axis=1)], axis=0)
        mz = lax.dot_general(bk.astype(BF16), rhs.astype(BF16), tn, preferred_element_type=F32)
        m = jnp.where(same_blk, mz[:, :LANES], 0.0) + jnp.where(eye, jnp.exp(ce), 0.0)
        return m, jnp.where(same_blk, mz[:, LANES:], 0.0)

    mz = _each(carry_terms, b, k, v, cum, cum_end, wu)
    return rw_s, y0_s, [t[0] for t in mz], [t[1] for t in mz]


def _rwkv_scan_kernel(r_ref, lw_ref, k_ref, v_ref, a_ref, b_ref, g_ref, gng, gnb, rkw, eones,
                      o_ref, state_ref, y_ref, *, tt):
    @pl.when(pl.program_id(0) == 0)
    def _():
        state_ref[...] = jnp.zeros(state_ref.shape, F32)

    rr = lax.broadcasted_iota(jnp.int32, (LANES, LANES), 0)
    cc = lax.broadcasted_iota(jnp.int32, (LANES, LANES), 1)
    same_blk = (rr // CHUNK) == (cc // CHUNK)
    strict = same_blk & ((cc % CHUNK) < (rr % CHUNK))
    incl = same_blk & ((cc % CHUNK) <= (rr % CHUNK))
    eye = rr == cc
    even_lane = lax.broadcasted_iota(jnp.int32, (CHUNK, LANES), 1) < CHUNK
    tr = lax.broadcasted_iota(jnp.int32, (CHUNK, CHUNK), 0)
    tc = lax.broadcasted_iota(jnp.int32, (CHUNK, CHUNK), 1)
    tri = jnp.where(tc <= tr, 1.0, 0.0).astype(BF16)
    masks = (even_lane, strict, incl, same_blk, eye, tri)

    batch = r_ref.shape[0]
    chains = [(bi, slice(p * LANES, (p + 1) * LANES))
              for bi in range(batch) for p in range(RWKV_DIM // LANES)]

    def chunk_body(c, carry):
        rows = pl.ds(pl.multiple_of(c * CHUNK, CHUNK), CHUNK)
        load = lambda ref: [ref[bi, rows, cs] for bi, cs in chains]
        rw_s, y0_s, m, z0 = _chunk_terms(load(r_ref), load(lw_ref), load(k_ref), load(v_ref),
                                         load(a_ref), load(b_ref), masks)
        ys = [_bdot(jnp.concatenate([rw_s[n], m[n]], axis=0), state_ref[n])
              for n in range(len(chains))]
        for n, (bi, cs) in enumerate(chains):
            y_ref[bi, rows, cs] = _unstack(ys[n][:LANES] + y0_s[n])
            state_ref[n] = ys[n][LANES:] + z0[n]
        return carry

    lax.fori_loop(0, tt // CHUNK, chunk_body, 0)

    e = eones[...]

    def headsum(x):
        hi = x.astype(BF16)
        lo = (x - hi.astype(F32)).astype(BF16)
        return (jnp.dot(hi, e, preferred_element_type=F32)
                + jnp.dot(lo, e, preferred_element_type=F32))

    for bi in range(batch):
        y = y_ref[bi]
        d = y - headsum(y) * (1.0 / RWKV_N)
        var = headsum(d * d) * (1.0 / RWKV_N)
        yn = d * lax.rsqrt(var + RWKV_GN_EPS) * gng[...] + gnb[...]
        yn = yn + headsum(r_ref[bi] * k_ref[bi] * rkw[...]) * v_ref[bi]
        g = g_ref[bi].astype(F32)
        o_ref[bi] = (yn * (g * jax.nn.sigmoid(g))).astype(o_ref.dtype)


def _rwkv_scan(r, lw, k, v, a, b, u, gng, gnb, rkw, eones, batch, seq, tt):
    to3 = lambda x: x.reshape(batch, seq, x.shape[-1])
    blk = pl.BlockSpec((batch, tt, RWKV_DIM), lambda i: (0, i, 0))
    full = lambda x: pl.BlockSpec(x.shape, lambda i: (0,) * x.ndim)
    out = pl.pallas_call(
        functools.partial(_rwkv_scan_kernel, tt=tt),
        grid=(seq // tt,),
        in_specs=[blk] * 6 + [pl.BlockSpec((batch, tt, RWKV_DIM), lambda i: (0, i, U_RG // RWKV_DIM)),
                              full(gng), full(gnb), full(rkw), full(eones)],
        out_specs=blk,
        out_shape=jax.ShapeDtypeStruct((batch, seq, RWKV_DIM), BF16),
        scratch_shapes=[pltpu.VMEM((batch * RWKV_DIM // LANES, LANES, LANES), F32),
                        pltpu.VMEM((batch, tt, RWKV_DIM), F32)],
        compiler_params=_cparams(("arbitrary",)),
        name="rwkv_scan",
    )(to3(r), to3(lw), to3(k), to3(v), to3(a), to3(b), to3(u), gng, gnb, rkw, eones)
    return out.reshape(batch * seq, RWKV_DIM)


def _outproj_kernel(yc, ym, yr, x_ref, w1, w2, w3, lg, lb, o_ref):
    sub = min(256, o_ref.shape[0])
    for r0 in range(0, o_ref.shape[0], sub):
        rs = slice(r0, r0 + sub)
        acc = jnp.dot(yc[rs, :], w1[...], preferred_element_type=F32)
        acc = acc + jnp.dot(ym[rs, :], w2[...], preferred_element_type=F32)
        acc = acc + jnp.dot(yr[rs, :], w3[...], preferred_element_type=F32)
        z = DEEPNORM_ALPHA * x_ref[rs, :] + acc
        mu = jnp.mean(z, -1, keepdims=True)
        d = z - mu
        var = jnp.mean(d * d, -1, keepdims=True)
        o_ref[rs, :] = d * lax.rsqrt(var + LN_EPS) * lg[...] + lb[...]


def _outproj(yc, ym, yr, x2, w1, w2, w3, lg, lb, tm):
    t = x2.shape[0]
    rblk = lambda d: pl.BlockSpec((tm, d), lambda i: (i, 0))
    full = lambda a: pl.BlockSpec(a.shape, lambda i: (0,) * a.ndim)
    return pl.pallas_call(
        _outproj_kernel,
        grid=(t // tm,),
        in_specs=[rblk(CONV_DIM), rblk(MLA_DIM), rblk(RWKV_DIM), rblk(D_MODEL),
                  full(w1), full(w2), full(w3), full(lg), full(lb)],
        out_specs=rblk(D_MODEL),
        out_shape=jax.ShapeDtypeStruct((t, D_MODEL), F32),
        compiler_params=_cparams(("parallel",)),
        name="outproj_ln",
    )(yc, ym, yr, x2, w1, w2, w3, lg, lb)


def _rot_half_cols(w):
    return jnp.concatenate([-w[:, QK_ROPE // 2:], w[:, :QK_ROPE // 2]], axis=1)


def _pack_w_in(w):
    o_ckv, o_kpe, o_mg, o_rc = 2560, 2816, 2880, 3904
    ckv = w[:, o_ckv:o_kpe]
    kpe = w[:, o_kpe:o_mg]
    mg = w[:, o_mg:o_rc]
    r = w[:, o_rc:o_rc + 512]
    wd = w[:, o_rc + 512:o_rc + 576]
    k = w[:, o_rc + 576:o_rc + 1088]
    v = w[:, o_rc + 1088:o_rc + 1600]
    ad = w[:, o_rc + 1600:o_rc + 1664]
    rg = w[:, o_rc + 1664:]
    return jnp.concatenate([w[:, :o_ckv], r, k, v, rg, mg, ckv, kpe, _rot_half_cols(kpe),
                            wd, ad], axis=1).astype(BF16)


def _pack_mu(mu):
    r, wd, k, v, ad = (mu[0:512], mu[512:576], mu[576:1088], mu[1088:1600], mu[1600:1664])
    row = lambda a: a.reshape(1, -1)
    return row(r), row(k), row(v), row(jnp.concatenate([wd, ad]))


def _pack_w_uq(w):
    cols = []
    for h in range(MLA_HEADS):
        wh = w[:, h * QK_HEAD:(h + 1) * QK_HEAD]
        pe = wh[:, QK_NOPE:]
        cols += [wh[:, :QK_NOPE], pe, _rot_half_cols(pe)]
    return jnp.concatenate(cols, axis=1).astype(BF16)


def _tile(n, pref):
    return pref if n % pref == 0 else n


def kernel(x, positions, w_in, conv_w, q_norm_g, w_uq, kv_norm_g, w_ukv, rwkv_mu, rwkv_w0,
           rwkv_w2, rwkv_a0, rwkv_a2, rwkv_k_k, rwkv_k_a, rwkv_r_k, rwkv_gn_g, rwkv_gn_b,
           w_out, ln_g, ln_b):
    batch, seq, _ = x.shape
    t = batch * seq
    row = lambda a: a.reshape(1, -1)

    inv_freq = ROPE_THETA ** (-jnp.arange(0, QK_ROPE, 2, dtype=F32) / QK_ROPE)
    invf4 = jnp.tile(inv_freq, 4).reshape(1, LANES)
    cs = _rope_table(positions.reshape(t, 1), invf4, _tile(t, 1024))

    head_of = jnp.arange(RWKV_DIM) // RWKV_N
    eones = (head_of[:, None] == head_of[None, :]).astype(BF16)
    zpad = jnp.zeros((DECAY_LORA, RWKV_DIM), F32)

    x2 = x.reshape(t, D_MODEL)
    for l in range(DEPTH):
        u = _inproj(x2, _pack_w_in(w_in[l]), _tile(t, 1024), 768)
        y_conv = _conv_branch(u, conv_w[l], seq, _tile(seq, 512))
        q, k, v = _mla_prep(u, cs, row(q_norm_g[l]), _pack_w_uq(w_uq[l]), row(kv_norm_g[l]),
                            w_ukv[l].astype(BF16), batch, seq, _tile(seq, 512))
        y_mla = _attention(q, k, v, u, seq, _tile(seq, 1024), 512)
        mu_r, mu_k, mu_v, mu_wa = _pack_mu(rwkv_mu[l])
        w2p = jnp.concatenate([rwkv_w2[l], zpad], axis=0).astype(BF16)
        a2p = jnp.concatenate([zpad, rwkv_a2[l]], axis=0).astype(BF16)
        rr, lw, kk, vv, aa, bb = _rwkv_prep(
            u, mu_r, mu_k, mu_v, mu_wa, row(rwkv_w0[l]), w2p, row(rwkv_a0[l]), a2p,
            row(rwkv_k_k[l]), row(rwkv_k_a[l]), eones, seq, _tile(seq, 512))
        y_rwkv = _rwkv_scan(rr, lw, kk, vv, aa, bb, u, row(rwkv_gn_g[l]), row(rwkv_gn_b[l]),
                            row(rwkv_r_k[l]), eones, batch, seq, _tile(seq, 512))
        wo = w_out[l].astype(BF16)
        x2 = _outproj(y_conv, y_mla, y_rwkv, x2,
                      wo[:CONV_DIM], wo[CONV_DIM:CONV_DIM + MLA_DIM], wo[CONV_DIM + MLA_DIM:],
                      row(ln_g[l]), row(ln_b[l]), _tile(t, 512))
    return x2.reshape(batch, seq, D_MODEL)
```

```python
import functools

import jax
import jax.numpy as jnp
from jax import lax
from jax.experimental import pallas as pl
from jax.experimental.pallas import tpu as pltpu

F32 = jnp.float32
BF16 = jnp.bfloat16

D_MODEL = 2048
CONV_DIM = 512
CONV_K = 3
MLA_HEADS = 8
QK_NOPE = 128
QK_ROPE = 64
QK_HEAD = QK_NOPE + QK_ROPE
V_DIM = 128
MLA_DIM = MLA_HEADS * V_DIM
Q_LORA = 512
KV_LORA = 256
ROPE_THETA = 10000.0
RWKV_HEADS = 8
RWKV_N = 64
RWKV_DIM = RWKV_HEADS * RWKV_N
DECAY_LORA = 64
A_LORA = 64
RWKV_GN_EPS = 64e-5
LN_EPS = 1e-5
RMS_EPS = 1e-6
DEPTH = 2
DEEPNORM_ALPHA = (2 * DEPTH) ** 0.25

U_CB, U_CC, U_CH, U_CG = 0, 512, 1024, 1536
U_CQ = 2048
U_R, U_K, U_V, U_RG = 2560, 3072, 3584, 4096
U_MG = 4608
U_CKV = 5632
U_KPE = 5888
U_WA = 6016
U_TOTAL = 6144

LANES = 128
PREV_ROWS = 16
CHUNK = 64
UNROLL = 2
VMEM_LIMIT = 56 * 1024 * 1024

LOG2E = 1.4426950408889634


def _cparams(sem):
    return pltpu.CompilerParams(dimension_semantics=sem, vmem_limit_bytes=VMEM_LIMIT)


def _bdot(a, b):
    return jnp.dot(a.astype(BF16), b.astype(BF16), preferred_element_type=F32)


def _inproj_kernel(x_ref, w_ref, o_ref, xb_ref):
    @pl.when(pl.program_id(1) == 0)
    def _():
        xb_ref[...] = x_ref[...].astype(BF16)

    o_ref[...] = jnp.dot(xb_ref[...], w_ref[...], preferred_element_type=F32).astype(o_ref.dtype)


def _inproj(x2, w_p, tm, tn):
    t = x2.shape[0]
    return pl.pallas_call(
        _inproj_kernel,
        grid=(t // tm, U_TOTAL // tn),
        in_specs=[pl.BlockSpec((tm, D_MODEL), lambda i, j: (i, 0)),
                  pl.BlockSpec((D_MODEL, tn), lambda i, j: (0, j))],
        out_specs=pl.BlockSpec((tm, tn), lambda i, j: (i, j)),
        out_shape=jax.ShapeDtypeStruct((t, U_TOTAL), BF16),
        scratch_shapes=[pltpu.VMEM((tm, D_MODEL), BF16)],
        compiler_params=_cparams(("parallel", "arbitrary")),
        name="inproj",
    )(x2, w_p)


def _shift_rows(x, prev, n, row):
    out = pltpu.roll(x, n, 0)
    for r in range(n):
        out = jnp.where(row == r, prev[PREV_ROWS - n + r:PREV_ROWS - n + r + 1], out)
    return out


def _conv_tile(cb, cc, ch, cg, ccp, chp, w, first):
    up = cc[...].astype(F32) * ch[...].astype(F32)
    prev = jnp.where(first, 0.0, ccp[...].astype(F32) * chp[...].astype(F32))
    row = lax.broadcasted_iota(jnp.int32, up.shape, 0)
    u1 = _shift_rows(up, prev, 1, row)
    u2 = _shift_rows(up, prev, 2, row)
    wv = w[...]
    y = wv[0:1] * u2 + wv[1:2] * u1 + wv[2:3] * up
    g = cg[...].astype(F32)
    return (cb[...].astype(F32) * y * (g * jax.nn.sigmoid(g))).astype(BF16)


def _rope_kernel(pos_ref, invf_ref, cs_ref):
    ang = pos_ref[...].astype(F32) * invf_ref[...]
    lane = lax.broadcasted_iota(jnp.int32, ang.shape, 1)
    cs_ref[...] = jnp.where(lane < QK_ROPE, jnp.cos(ang), jnp.sin(ang))


def _rope_table(pos2, invf4, tm):
    t = pos2.shape[0]
    return pl.pallas_call(
        _rope_kernel,
        grid=(t // tm,),
        in_specs=[pl.BlockSpec((tm, 1), lambda i: (i, 0)),
                  pl.BlockSpec((1, LANES), lambda i: (0, 0))],
        out_specs=pl.BlockSpec((tm, LANES), lambda i: (i, 0)),
        out_shape=jax.ShapeDtypeStruct((t, LANES), F32),
        compiler_params=_cparams(("parallel",)),
        name="rope_table",
    )(pos2, invf4)


def _rms(x, g):
    return x * lax.rsqrt(jnp.mean(x * x, -1, keepdims=True) + RMS_EPS) * g


def _mla_prep_kernel(cq_ref, ckv_ref, kp_ref, cs_ref, qg_ref, wq_ref, kg_ref, wkv_ref,
                     q_out, k_out, v_out, *, qscale):
    cs = cs_ref[...]
    qf = _bdot(_rms(cq_ref[...].astype(F32), qg_ref[...]), wq_ref[...])
    kvf = _bdot(_rms(ckv_ref[...].astype(F32), kg_ref[...]), wkv_ref[...])
    kp = kp_ref[...].astype(F32) * cs
    kpe = (kp + pltpu.roll(kp, QK_ROPE, 1))[:, :QK_ROPE].astype(BF16)
    for h in range(MLA_HEADS):
        c0 = 2 * LANES * h
        pp = qf[:, c0 + LANES:c0 + 2 * LANES] * cs
        pe = pp + pltpu.roll(pp, QK_ROPE, 1)
        q_out[h, :, 0:QK_NOPE] = (qf[:, c0:c0 + LANES] * qscale).astype(BF16)
        q_out[h, :, QK_NOPE:QK_HEAD] = (pe[:, :QK_ROPE] * qscale).astype(BF16)
        k_out[h, :, 0:QK_NOPE] = kvf[:, c0:c0 + LANES].astype(BF16)
        k_out[h, :, QK_NOPE:QK_HEAD] = kpe
        v_out[h, :, 0:V_DIM] = kvf[:, c0 + LANES:c0 + 2 * LANES].astype(BF16)
        v_out[h, :, V_DIM:2 * V_DIM] = jnp.ones((kvf.shape[0], V_DIM), BF16)


def _mla_prep(u, cs, qg, wq_p, kg, wkv, batch, seq, tm):
    t = u.shape[0]
    nt = seq // tm
    hblk = lambda d: pl.BlockSpec((None, MLA_HEADS, tm, d), lambda i: (i // nt, 0, i % nt, 0))
    full = lambda a: pl.BlockSpec(a.shape, lambda i: (0,) * a.ndim)
    qscale = QK_HEAD ** -0.5 * LOG2E
    return pl.pallas_call(
        functools.partial(_mla_prep_kernel, qscale=qscale),
        grid=(t // tm,),
        in_specs=[pl.BlockSpec((tm, Q_LORA), lambda i: (i, U_CQ // Q_LORA)),
                  pl.BlockSpec((tm, KV_LORA), lambda i: (i, U_CKV // KV_LORA)),
                  pl.BlockSpec((tm, LANES), lambda i: (i, U_KPE // LANES)),
                  pl.BlockSpec((tm, LANES), lambda i: (i, 0)),
                  full(qg), full(wq_p), full(kg), full(wkv)],
        out_specs=[hblk(QK_HEAD), hblk(QK_HEAD), hblk(2 * V_DIM)],
        out_shape=[jax.ShapeDtypeStruct((batch, MLA_HEADS, seq, QK_HEAD), BF16),
                   jax.ShapeDtypeStruct((batch, MLA_HEADS, seq, QK_HEAD), BF16),
                   jax.ShapeDtypeStruct((batch, MLA_HEADS, seq, 2 * V_DIM), BF16)],
        compiler_params=_cparams(("parallel",)),
        name="mla_prep",
    )(u, u, u, cs, qg, wq_p, kg, wkv)


def _attn_kernel(q_ref, k_ref, v_ref, g_ref, o_ref, m_ref, acc_ref, s0, s1, p0, p1, a0, a1, x0, x1, *, tq, sub):
    tk = tq // 2
    i = pl.program_id(2)
    s_scr, p_scr, a_scr, x_scr = (s0, s1), (p0, p1), (a0, a1), (x0, x1)
    lo, hi, full = slice(0, tk), slice(tk, tq), slice(0, tq)
    nt = (((1,), (1,)), ((), ()))

    m_ref[...] = jnp.full(m_ref.shape, -jnp.inf, F32)
    acc_ref[...] = jnp.zeros(acc_ref.shape, F32)
    p1[...] = jnp.zeros(p1.shape, BF16)
    a1[...] = jnp.ones(a1.shape, F32)

    def qk(t, slot, rows):
        kb = k_ref[pl.ds(pl.multiple_of(t * tk, tk), tk), :]
        for r0 in range(rows.start, rows.stop, sub):
            rs = slice(r0, r0 + sub)
            s = lax.dot_general(q_ref[rs, :], kb, nt, preferred_element_type=F32)
            s_scr[slot][rs, :] = s
            x_scr[slot][rs, :] = jnp.broadcast_to(jnp.max(s, axis=1, keepdims=True), (sub, LANES))

    def softmax(slot, rows, causal):
        for r0 in range(rows.start, rows.stop, sub):
            rs = slice(r0, r0 + sub)
            s = s_scr[slot][rs, :]
            if causal:
                r = lax.broadcasted_iota(jnp.int32, s.shape, 0) + (r0 - rows.start)
                c = lax.broadcasted_iota(jnp.int32, s.shape, 1)
                s = jnp.where(c <= r, s, -jnp.inf)
                s_max = jnp.max(s, axis=1, keepdims=True)
            else:
                s_max = x_scr[slot][rs, :]
            m_prev = m_ref[rs, :]
            m_new = jnp.maximum(m_prev, s_max)
            p_scr[slot][rs, :] = jnp.exp2(s - pltpu.repeat(m_new, tk // LANES, 1)).astype(BF16)
            a_scr[slot][rs, :] = jnp.exp2(m_prev - m_new)
            m_ref[rs, :] = m_new

    def pv(t, slot, rows):
        vb = v_ref[pl.ds(pl.multiple_of(t * tk, tk), tk), :]
        acc_ref[rows, :] = (pltpu.repeat(a_scr[slot][rows, :], 2, 1) * acc_ref[rows, :]
                            + jnp.dot(p_scr[slot][rows, :], vb, preferred_element_type=F32))

    def step(t, slot):
        qk(t + 1, 1 - slot, full)
        pv(jnp.maximum(t - 1, 0), 1 - slot, full)
        softmax(slot, full, False)

    def pair(tt):
        step(2 * tt, 0)
        step(2 * tt + 1, 1)

    qk(0, 0, full)
    one, two = i & 1, i & 2

    @pl.when(one == 1)
    def _():
        pair(0)

    @pl.when(two == 2)
    def _():
        pair(one)
        pair(one + 1)

    def four_pairs(n, carry):
        for j in range(4):
            pair(one + two + 4 * n + j)
        return carry

    lax.fori_loop(0, i // 4, four_pairs, 0)
    d = 2 * i
    qk(d + 1, 1, hi)
    pv(jnp.maximum(d - 1, 0), 1, full)
    softmax(0, lo, True)
    softmax(0, hi, False)
    softmax(1, hi, True)
    pv(d, 0, full)
    pv(d + 1, 1, hi)
    g = g_ref[...].astype(F32)
    acc = acc_ref[...]
    o_ref[...] = (acc[:, :V_DIM] / acc[:, V_DIM:] * (g * jax.nn.sigmoid(g))).astype(o_ref.dtype)


def _attention(q, k, v_aug, u, seq, tq, sub):
    batch = q.shape[0]
    t = batch * seq
    nq = seq // tq
    return pl.pallas_call(
        functools.partial(_attn_kernel, tq=tq, sub=sub),
        grid=(batch, MLA_HEADS, nq),
        in_specs=[pl.BlockSpec((None, None, tq, QK_HEAD), lambda b, h, i: (b, h, i, 0)),
                  pl.BlockSpec((None, None, seq, QK_HEAD), lambda b, h, i: (b, h, 0, 0)),
                  pl.BlockSpec((None, None, seq, 2 * V_DIM), lambda b, h, i: (b, h, 0, 0)),
                  pl.BlockSpec((tq, V_DIM), lambda b, h, i: (b * nq + i, U_MG // V_DIM + h))],
        out_specs=pl.BlockSpec((tq, V_DIM), lambda b, h, i: (b * nq + i, h)),
        out_shape=jax.ShapeDtypeStruct((t, MLA_DIM), BF16),
        scratch_shapes=[pltpu.VMEM((tq, LANES), F32), pltpu.VMEM((tq, 2 * V_DIM), F32),
                        pltpu.VMEM((tq, tq // 2), F32), pltpu.VMEM((tq, tq // 2), F32),
                        pltpu.VMEM((tq, tq // 2), BF16), pltpu.VMEM((tq, tq // 2), BF16),
                        pltpu.VMEM((tq, LANES), F32), pltpu.VMEM((tq, LANES), F32),
                        pltpu.VMEM((tq, LANES), F32), pltpu.VMEM((tq, LANES), F32)],
        compiler_params=_cparams(("parallel", "parallel", "arbitrary")),
        name="mla_attention",
    )(q, k, v_aug, u)


DECAY_SCALE = 0.6065306597126334


def _rwkv_prep_kernel(r_ref, k_ref, v_ref, wa_ref, rp_ref, kp_ref, vp_ref, wap_ref,
                      mu_r, mu_k, mu_v, mu_wa, w0, w2p, a0, a2p, kkw, kaw, eones,
                      r_o, lw_o, k_o, v_o, a_o, b_o, *, tiles_per_seq):
    first = (pl.program_id(0) % tiles_per_seq) == 0
    row = lax.broadcasted_iota(jnp.int32, r_ref.shape, 0)
    row_wa = lax.broadcasted_iota(jnp.int32, wa_ref.shape, 0)

    def lerp(x_ref, p_ref, mu, rw):
        x = x_ref[...].astype(F32)
        prev = jnp.where(first, 0.0, p_ref[...].astype(F32))
        return x + (_shift_rows(x, prev, 1, rw) - x) * mu[...]

    r = lerp(r_ref, rp_ref, mu_r, row)
    k = lerp(k_ref, kp_ref, mu_k, row)
    v = lerp(v_ref, vp_ref, mu_v, row)
    wa = lerp(wa_ref, wap_ref, mu_wa, row_wa)
    z = w0[...] + _bdot(jnp.tanh(wa), w2p[...])
    a_sig = jax.nn.sigmoid(a0[...] + _bdot(wa, a2p[...]))
    kk = k * kkw[...]
    ss = _bdot(kk * kk, eones[...])
    kk = kk * jnp.minimum(lax.rsqrt(ss), 1e12)
    r_o[...] = r
    lw_o[...] = -DECAY_SCALE * jax.nn.sigmoid(z)
    k_o[...] = k * (1.0 + (a_sig - 1.0) * kaw[...])
    v_o[...] = v
    a_o[...] = -kk
    b_o[...] = kk * a_sig


def _rwkv_prep(u, mu_r, mu_k, mu_v, mu_wa, w0, w2p, a0, a2p, kkw, kaw, eones, seq, tm):
    t = u.shape[0]
    cblk = lambda c: pl.BlockSpec((tm, RWKV_DIM), lambda i, c=c: (i, c))
    prow = lambda i: jnp.maximum(i * (tm // PREV_ROWS) - 1, 0)
    pblk = lambda c: pl.BlockSpec((PREV_ROWS, RWKV_DIM), lambda i, c=c: (prow(i), c))
    full = lambda a: pl.BlockSpec(a.shape, lambda i: (0,) * a.ndim)
    out = jax.ShapeDtypeStruct((t, RWKV_DIM), F32)
    oblk = pl.BlockSpec((tm, RWKV_DIM), lambda i: (i, 0))
    return pl.pallas_call(
        functools.partial(_rwkv_prep_kernel, tiles_per_seq=seq // tm),
        grid=(t // tm,),
        in_specs=[cblk(U_R // 512), cblk(U_K // 512), cblk(U_V // 512),
                  pl.BlockSpec((tm, LANES), lambda i: (i, U_WA // LANES)),
                  pblk(U_R // 512), pblk(U_K // 512), pblk(U_V // 512),
                  pl.BlockSpec((PREV_ROWS, LANES), lambda i: (prow(i), U_WA // LANES)),
                  full(mu_r), full(mu_k), full(mu_v), full(mu_wa), full(w0), full(w2p),
                  full(a0), full(a2p), full(kkw), full(kaw), full(eones)],
        out_specs=[oblk] * 6,
        out_shape=[out] * 6,
        compiler_params=_cparams(("parallel",)),
        name="rwkv_prep",
    )(u, u, u, u, u, u, u, u, mu_r, mu_k, mu_v, mu_wa, w0, w2p, a0, a2p, kkw, kaw, eones)


def _stack(x, even_lane):
    return jnp.concatenate([jnp.where(even_lane, x, 0.0), jnp.where(even_lane, 0.0, x)], axis=0)


def _unstack(x):
    return x[:CHUNK] + x[CHUNK:]


def _each(fn, *lists):
    return [fn(*args) for args in zip(*lists)]


def _chunk_terms(r, lw, k, v, a, b, masks):
    even_lane, strict, incl, same_blk, eye, tri = masks
    nt = (((1,), (1,)), ((), ()))
    tn = (((0,), (0,)), ((), ()))

    def cumsum(x):
        hi = x.astype(BF16)
        lo = (x - hi.astype(F32)).astype(BF16)
        return (jnp.dot(tri, hi, preferred_element_type=F32)
                + jnp.dot(tri, lo, preferred_element_type=F32))

    cum = _each(cumsum, lw)
    cum_end = _each(lambda c: c[CHUNK - 1:CHUNK], cum)
    at_s = _each(lambda x, c, w: _stack(x * jnp.exp(c - w), even_lane), a, cum, lw)
    rt_s = _each(lambda x, c: _stack(x * jnp.exp(c), even_lane), r, cum)
    v_s = _each(lambda x: _stack(x, even_lane), v)
    bt = _each(lambda x, c: (x * jnp.exp(-c)).astype(BF16), b, cum)
    kt = _each(lambda x, c: (x * jnp.exp(-c)).astype(BF16), k, cum)
    g = _each(lambda x, y, p, q: lax.dot_general(
        jnp.concatenate([x, y], axis=0).astype(BF16), jnp.concatenate([p, p, q, q], axis=0), nt,
        preferred_element_type=F32), at_s, rt_s, bt, kt)
    x = _each(lambda m: jnp.where(strict, m[:LANES, :LANES], 0.0), g)
    a_ak = _each(lambda m: jnp.where(strict, m[:LANES, LANES:], 0.0), g)
    a_r = _each(lambda m: jnp.concatenate([jnp.where(incl, m[LANES:, :LANES], 0.0),
                                           jnp.where(incl, m[LANES:, LANES:], 0.0)], axis=1), g)
    akv = _each(_bdot, a_ak, v_s)

    tinv = _each(lambda m: jnp.where(eye, 1.0, 0.0) + m, x)
    n = 2
    while n < CHUNK:
        x = _each(lambda m: _bdot(m, m), x)
        tinv = _each(lambda t, m: t + _bdot(t, m), tinv, x)
        n *= 2

    wu = _each(lambda t, p, q: _bdot(t, jnp.concatenate([p, q], axis=1)), tinv, at_s, akv)
    zeros = jnp.zeros((LANES, LANES), F32)
    top = _each(lambda m, w, q: _bdot(m, jnp.concatenate(
        [w, jnp.concatenate([zeros, q], axis=1)], axis=0)), a_r, wu, v_s)
    rw_s = _each(lambda x, t: x + t[:, :LANES], rt_s, top)
    y0_s = _each(lambda t: t[:, LANES:], top)

    def carry_terms(bb, kk, vv, c, ce, w):
        to_end = jnp.exp(ce - c)
        bk = jnp.concatenate([bb * to_end, kk * to_end], axis=0)
        rhs = jnp.concatenate(
            [jnp.concatenate([_unstack(w[:, :LANES]), _unstack(w[:, LANES:])], axis=1),
             jnp.concatenate([jnp.zeros((CHUNK, LANES), F32), vv], axis=1)], axis=0)
        mz = lax.dot_general(bk.astype(BF16), rhs.astype(BF16), tn, preferred_element_type=F32)
        m = jnp.where(same_blk, mz[:, :LANES], 0.0) + jnp.where(eye, jnp.exp(ce), 0.0)
        return m, jnp.where(same_blk, mz[:, LANES:], 0.0)

    mz = _each(carry_terms, b, k, v, cum, cum_end, wu)
    return rw_s, y0_s, [t[0] for t in mz], [t[1] for t in mz]


def _rwkv_scan_kernel(r_ref, lw_ref, k_ref, v_ref, a_ref, b_ref, g_ref, gng, gnb, rkw, eones,
                      o_ref, state_ref, y_ref, *, tt):
    @pl.when(pl.program_id(0) == 0)
    def _():
        state_ref[...] = jnp.zeros(state_ref.shape, F32)

    rr = lax.broadcasted_iota(jnp.int32, (LANES, LANES), 0)
    cc = lax.broadcasted_iota(jnp.int32, (LANES, LANES), 1)
    same_blk = (rr // CHUNK) == (cc // CHUNK)
    strict = same_blk & ((cc % CHUNK) < (rr % CHUNK))
    incl = same_blk & ((cc % CHUNK) <= (rr % CHUNK))
    eye = rr == cc
    even_lane = lax.broadcasted_iota(jnp.int32, (CHUNK, LANES), 1) < CHUNK
    tr = lax.broadcasted_iota(jnp.int32, (CHUNK, CHUNK), 0)
    tc = lax.broadcasted_iota(jnp.int32, (CHUNK, CHUNK), 1)
    tri = jnp.where(tc <= tr, 1.0, 0.0).astype(BF16)
    masks = (even_lane, strict, incl, same_blk, eye, tri)

    batch = r_ref.shape[0]
    chains = [(bi, slice(p * LANES, (p + 1) * LANES))
              for bi in range(batch) for p in range(RWKV_DIM // LANES)]

    nch = len(chains)

    def chunk_body(c, carry):
        rows = [pl.ds(pl.multiple_of((c * UNROLL + j) * CHUNK, CHUNK), CHUNK) for j in range(UNROLL)]
        load = lambda ref: [ref[bi, rw, cs] for rw in rows for bi, cs in chains]
        rw_s, y0_s, m, z0 = _chunk_terms(load(r_ref), load(lw_ref), load(k_ref), load(v_ref),
                                         load(a_ref), load(b_ref), masks)
        st = [state_ref[n] for n in range(nch)]
        for j in range(UNROLL):
            ys = [_bdot(jnp.concatenate([rw_s[j * nch + n], m[j * nch + n]], axis=0), st[n])
                  for n in range(nch)]
            for n, (bi, cs) in enumerate(chains):
                y_ref[bi, rows[j], cs] = _unstack(ys[n][:LANES] + y0_s[j * nch + n])
            st = [ys[n][LANES:] + z0[j * nch + n] for n in range(nch)]
        for n in range(nch):
            state_ref[n] = st[n]
        return carry

    lax.fori_loop(0, tt // (CHUNK * UNROLL), chunk_body, 0)

    e = eones[...]

    def headsum(x):
        return jnp.dot(x.astype(BF16), e, preferred_element_type=F32)

    for bi in range(batch):
        y = y_ref[bi]
        d = y - headsum(y) * (1.0 / RWKV_N)
        var = headsum(d * d) * (1.0 / RWKV_N)
        yn = d * lax.rsqrt(var + RWKV_GN_EPS) * gng[...] + gnb[...]
        yn = yn + headsum(r_ref[bi] * k_ref[bi] * rkw[...]) * v_ref[bi]
        g = g_ref[bi].astype(F32)
        o_ref[bi] = (yn * (g * jax.nn.sigmoid(g))).astype(o_ref.dtype)


def _rwkv_scan(r, lw, k, v, a, b, u, gng, gnb, rkw, eones, batch, seq, tt):
    to3 = lambda x: x.reshape(batch, seq, x.shape[-1])
    blk = pl.BlockSpec((batch, tt, RWKV_DIM), lambda i: (0, i, 0))
    full = lambda x: pl.BlockSpec(x.shape, lambda i: (0,) * x.ndim)
    out = pl.pallas_call(
        functools.partial(_rwkv_scan_kernel, tt=tt),
        grid=(seq // tt,),
        in_specs=[blk] * 6 + [pl.BlockSpec((batch, tt, RWKV_DIM), lambda i: (0, i, U_RG // RWKV_DIM)),
                              full(gng), full(gnb), full(rkw), full(eones)],
        out_specs=blk,
        out_shape=jax.ShapeDtypeStruct((batch, seq, RWKV_DIM), BF16),
        scratch_shapes=[pltpu.VMEM((batch * RWKV_DIM // LANES, LANES, LANES), F32),
                        pltpu.VMEM((batch, tt, RWKV_DIM), F32)],
        compiler_params=_cparams(("arbitrary",)),
        name="rwkv_scan",
    )(to3(r), to3(lw), to3(k), to3(v), to3(a), to3(b), to3(u), gng, gnb, rkw, eones)
    return out.reshape(batch * seq, RWKV_DIM)


def _outproj_kernel(cb, cc, ch, cg, ccp, chp, cw, ym, yr, x_ref, w1, w2, w3, lg, lb, o_ref, yc,
                    *, tiles_per_seq):
    first = (pl.program_id(0) % tiles_per_seq) == 0
    yc[...] = _conv_tile(cb, cc, ch, cg, ccp, chp, cw, first)
    sub = min(256, o_ref.shape[0])
    for r0 in range(0, o_ref.shape[0], sub):
        rs = slice(r0, r0 + sub)
        acc = jnp.dot(yc[rs, :], w1[...], preferred_element_type=F32)
        acc = acc + jnp.dot(ym[rs, :], w2[...], preferred_element_type=F32)
        acc = acc + jnp.dot(yr[rs, :], w3[...], preferred_element_type=F32)
        z = DEEPNORM_ALPHA * x_ref[rs, :] + acc
        mu = jnp.mean(z, -1, keepdims=True)
        d = z - mu
        var = jnp.mean(d * d, -1, keepdims=True)
        o_ref[rs, :] = d * lax.rsqrt(var + LN_EPS) * lg[...] + lb[...]


def _outproj(u, conv_w, ym, yr, x2, w1, w2, w3, lg, lb, seq, tm):
    t = x2.shape[0]
    rblk = lambda d: pl.BlockSpec((tm, d), lambda i: (i, 0))
    full = lambda a: pl.BlockSpec(a.shape, lambda i: (0,) * a.ndim)
    cblk = lambda c: pl.BlockSpec((tm, CONV_DIM), lambda i, c=c: (i, c))
    pblk = lambda c: pl.BlockSpec((PREV_ROWS, CONV_DIM),
                                  lambda i, c=c: (jnp.maximum(i * (tm // PREV_ROWS) - 1, 0), c))
    return pl.pallas_call(
        functools.partial(_outproj_kernel, tiles_per_seq=seq // tm),
        grid=(t // tm,),
        in_specs=[cblk(U_CB // 512), cblk(U_CC // 512), cblk(U_CH // 512), cblk(U_CG // 512),
                  pblk(U_CC // 512), pblk(U_CH // 512), full(conv_w),
                  rblk(MLA_DIM), rblk(RWKV_DIM), rblk(D_MODEL),
                  full(w1), full(w2), full(w3), full(lg), full(lb)],
        out_specs=rblk(D_MODEL),
        out_shape=jax.ShapeDtypeStruct((t, D_MODEL), F32),
        scratch_shapes=[pltpu.VMEM((tm, CONV_DIM), BF16)],
        compiler_params=_cparams(("parallel",)),
        name="outproj_ln",
    )(u, u, u, u, u, u, conv_w, ym, yr, x2, w1, w2, w3, lg, lb)


def _rot_half_cols(w):
    return jnp.concatenate([-w[:, QK_ROPE // 2:], w[:, :QK_ROPE // 2]], axis=1)


def _pack_w_in(w):
    o_ckv, o_kpe, o_mg, o_rc = 2560, 2816, 2880, 3904
    ckv = w[:, o_ckv:o_kpe]
    kpe = w[:, o_kpe:o_mg]
    mg = w[:, o_mg:o_rc]
    r = w[:, o_rc:o_rc + 512]
    wd = w[:, o_rc + 512:o_rc + 576]
    k = w[:, o_rc + 576:o_rc + 1088]
    v = w[:, o_rc + 1088:o_rc + 1600]
    ad = w[:, o_rc + 1600:o_rc + 1664]
    rg = w[:, o_rc + 1664:]
    return jnp.concatenate([w[:, :o_ckv], r, k, v, rg, mg, ckv, kpe, _rot_half_cols(kpe),
                            wd, ad], axis=1).astype(BF16)


def _pack_mu(mu):
    r, wd, k, v, ad = (mu[0:512], mu[512:576], mu[576:1088], mu[1088:1600], mu[1600:1664])
    row = lambda a: a.reshape(1, -1)
    return row(r), row(k), row(v), row(jnp.concatenate([wd, ad]))


def _pack_w_uq(w):
    cols = []
    for h in range(MLA_HEADS):
        wh = w[:, h * QK_HEAD:(h + 1) * QK_HEAD]
        pe = wh[:, QK_NOPE:]
        cols += [wh[:, :QK_NOPE], pe, _rot_half_cols(pe)]
    return jnp.concatenate(cols, axis=1).astype(BF16)


def _tile(n, pref):
    return pref if n % pref == 0 else n


def kernel(x, positions, w_in, conv_w, q_norm_g, w_uq, kv_norm_g, w_ukv, rwkv_mu, rwkv_w0,
           rwkv_w2, rwkv_a0, rwkv_a2, rwkv_k_k, rwkv_k_a, rwkv_r_k, rwkv_gn_g, rwkv_gn_b,
           w_out, ln_g, ln_b):
    batch, seq, _ = x.shape
    t = batch * seq
    row = lambda a: a.reshape(1, -1)

    inv_freq = ROPE_THETA ** (-jnp.arange(0, QK_ROPE, 2, dtype=F32) / QK_ROPE)
    invf4 = jnp.tile(inv_freq, 4).reshape(1, LANES)
    cs = _rope_table(positions.reshape(t, 1), invf4, _tile(t, 1024))

    head_of = jnp.arange(RWKV_DIM) // RWKV_N
    eones = (head_of[:, None] == head_of[None, :]).astype(BF16)
    zpad = jnp.zeros((DECAY_LORA, RWKV_DIM), F32)

    x2 = x.reshape(t, D_MODEL)
    for l in range(DEPTH):
        u = _inproj(x2, _pack_w_in(w_in[l]), _tile(t, 1024), 768)
        q, k, v = _mla_prep(u, cs, row(q_norm_g[l]), _pack_w_uq(w_uq[l]), row(kv_norm_g[l]),
                            w_ukv[l].astype(BF16), batch, seq, _tile(seq, 512))
        y_mla = _attention(q, k, v, u, seq, _tile(seq, 1024), 512)
        mu_r, mu_k, mu_v, mu_wa = _pack_mu(rwkv_mu[l])
        w2p = jnp.concatenate([rwkv_w2[l], zpad], axis=0).astype(BF16)
        a2p = jnp.concatenate([zpad, rwkv_a2[l]], axis=0).astype(BF16)
        rr, lw, kk, vv, aa, bb = _rwkv_prep(
            u, mu_r, mu_k, mu_v, mu_wa, row(rwkv_w0[l]), w2p, row(rwkv_a0[l]), a2p,
            row(rwkv_k_k[l]), row(rwkv_k_a[l]), eones, seq, _tile(seq, 512))
        y_rwkv = _rwkv_scan(rr, lw, kk, vv, aa, bb, u, row(rwkv_gn_g[l]), row(rwkv_gn_b[l]),
                            row(rwkv_r_k[l]), eones, batch, seq, _tile(seq, 512))
        wo = w_out[l].astype(BF16)
        x2 = _outproj(u, conv_w[l], y_mla, y_rwkv, x2,
                      wo[:CONV_DIM], wo[CONV_DIM:CONV_DIM + MLA_DIM], wo[CONV_DIM + MLA_DIM:],
                      row(ln_g[l]), row(ln_b[l]), seq, _tile(seq, 512))
    return x2.reshape(batch, seq, D_MODEL)
```

```python
import functools

import jax
import jax.numpy as jnp
from jax import lax
from jax.experimental import pallas as pl
from jax.experimental.pallas import tpu as pltpu

F32 = jnp.float32
BF16 = jnp.bfloat16

D_MODEL = 2048
CONV_DIM = 512
CONV_K = 3
MLA_HEADS = 8
QK_NOPE = 128
QK_ROPE = 64
QK_HEAD = QK_NOPE + QK_ROPE
V_DIM = 128
MLA_DIM = MLA_HEADS * V_DIM
Q_LORA = 512
KV_LORA = 256
ROPE_THETA = 10000.0
RWKV_HEADS = 8
RWKV_N = 64
RWKV_DIM = RWKV_HEADS * RWKV_N
DECAY_LORA = 64
A_LORA = 64
RWKV_GN_EPS = 64e-5
LN_EPS = 1e-5
RMS_EPS = 1e-6
DEPTH = 2
DEEPNORM_ALPHA = (2 * DEPTH) ** 0.25

U_CB, U_CC, U_CH, U_CG = 0, 512, 1024, 1536
U_CQ = 2048
U_R, U_K, U_V, U_RG = 2560, 3072, 3584, 4096
U_MG = 4608
U_CKV = 5632
U_KPE = 5888
U_WA = 6016
U_TOTAL = 6144

LANES = 128
PREV_ROWS = 16
CHUNK = 64
UNROLL = 2
VMEM_LIMIT = 56 * 1024 * 1024

LOG2E = 1.4426950408889634


def _cparams(sem):
    return pltpu.CompilerParams(dimension_semantics=sem, vmem_limit_bytes=VMEM_LIMIT)


def _bdot(a, b):
    return jnp.dot(a.astype(BF16), b.astype(BF16), preferred_element_type=F32)


def _inproj_kernel(x_ref, w_ref, o_ref, xb_ref):
    @pl.when(pl.program_id(1) == 0)
    def _():
        xb_ref[...] = x_ref[...].astype(BF16)

    o_ref[...] = jnp.dot(xb_ref[...], w_ref[...], preferred_element_type=F32).astype(o_ref.dtype)


def _inproj(x2, w_p, tm, tn):
    t = x2.shape[0]
    return pl.pallas_call(
        _inproj_kernel,
        grid=(t // tm, U_TOTAL // tn),
        in_specs=[pl.BlockSpec((tm, D_MODEL), lambda i, j: (i, 0)),
                  pl.BlockSpec((D_MODEL, tn), lambda i, j: (0, j))],
        out_specs=pl.BlockSpec((tm, tn), lambda i, j: (i, j)),
        out_shape=jax.ShapeDtypeStruct((t, U_TOTAL), BF16),
        scratch_shapes=[pltpu.VMEM((tm, D_MODEL), BF16)],
        compiler_params=_cparams(("parallel", "arbitrary")),
        name="inproj",
    )(x2, w_p)


def _shift_rows(x, prev, n, row):
    out = pltpu.roll(x, n, 0)
    for r in range(n):
        out = jnp.where(row == r, prev[PREV_ROWS - n + r:PREV_ROWS - n + r + 1], out)
    return out


def _conv_tile(cb, cc, ch, cg, ccp, chp, w, first):
    up = cc[...].astype(F32) * ch[...].astype(F32)
    prev = jnp.where(first, 0.0, ccp[...].astype(F32) * chp[...].astype(F32))
    row = lax.broadcasted_iota(jnp.int32, up.shape, 0)
    u1 = _shift_rows(up, prev, 1, row)
    u2 = _shift_rows(up, prev, 2, row)
    wv = w[...]
    y = wv[0:1] * u2 + wv[1:2] * u1 + wv[2:3] * up
    g = cg[...].astype(F32)
    return (cb[...].astype(F32) * y * (g * jax.nn.sigmoid(g))).astype(BF16)


def _rope_kernel(pos_ref, invf_ref, cs_ref):
    ang = pos_ref[...].astype(F32) * invf_ref[...]
    lane = lax.broadcasted_iota(jnp.int32, ang.shape, 1)
    cs_ref[...] = jnp.where(lane < QK_ROPE, jnp.cos(ang), jnp.sin(ang))


def _rope_table(pos2, invf4, tm):
    t = pos2.shape[0]
    return pl.pallas_call(
        _rope_kernel,
        grid=(t // tm,),
        in_specs=[pl.BlockSpec((tm, 1), lambda i: (i, 0)),
                  pl.BlockSpec((1, LANES), lambda i: (0, 0))],
        out_specs=pl.BlockSpec((tm, LANES), lambda i: (i, 0)),
        out_shape=jax.ShapeDtypeStruct((t, LANES), F32),
        compiler_params=_cparams(("parallel",)),
        name="rope_table",
    )(pos2, invf4)


def _rms(x, g):
    return x * lax.rsqrt(jnp.mean(x * x, -1, keepdims=True) + RMS_EPS) * g


def _mla_prep_kernel(cq_ref, ckv_ref, kp_ref, cs_ref, qg_ref, wq_ref, kg_ref, wkv_ref,
                     q_out, k_out, v_out, *, qscale):
    cs = cs_ref[...]
    qf = _bdot(_rms(cq_ref[...].astype(F32), qg_ref[...]), wq_ref[...])
    kvf = _bdot(_rms(ckv_ref[...].astype(F32), kg_ref[...]), wkv_ref[...])
    kp = kp_ref[...].astype(F32) * cs
    kpe = (kp + pltpu.roll(kp, QK_ROPE, 1))[:, :QK_ROPE].astype(BF16)
    for h in range(MLA_HEADS):
        c0 = 2 * LANES * h
        pp = qf[:, c0 + LANES:c0 + 2 * LANES] * cs
        pe = pp + pltpu.roll(pp, QK_ROPE, 1)
        q_out[h, :, 0:QK_NOPE] = (qf[:, c0:c0 + LANES] * qscale).astype(BF16)
        q_out[h, :, QK_NOPE:QK_HEAD] = (pe[:, :QK_ROPE] * qscale).astype(BF16)
        k_out[h, :, 0:QK_NOPE] = kvf[:, c0:c0 + LANES].astype(BF16)
        k_out[h, :, QK_NOPE:QK_HEAD] = kpe
        v_out[h, :, 0:V_DIM] = kvf[:, c0 + LANES:c0 + 2 * LANES].astype(BF16)
        v_out[h, :, V_DIM:2 * V_DIM] = jnp.ones((kvf.shape[0], V_DIM), BF16)


def _mla_prep(u, cs, qg, wq_p, kg, wkv, batch, seq, tm):
    t = u.shape[0]
    nt = seq // tm
    hblk = lambda d: pl.BlockSpec((None, MLA_HEADS, tm, d), lambda i: (i // nt, 0, i % nt, 0))
    full = lambda a: pl.BlockSpec(a.shape, lambda i: (0,) * a.ndim)
    qscale = QK_HEAD ** -0.5 * LOG2E
    return pl.pallas_call(
        functools.partial(_mla_prep_kernel, qscale=qscale),
        grid=(t // tm,),
        in_specs=[pl.BlockSpec((tm, Q_LORA), lambda i: (i, U_CQ // Q_LORA)),
                  pl.BlockSpec((tm, KV_LORA), lambda i: (i, U_CKV // KV_LORA)),
                  pl.BlockSpec((tm, LANES), lambda i: (i, U_KPE // LANES)),
                  pl.BlockSpec((tm, LANES), lambda i: (i, 0)),
                  full(qg), full(wq_p), full(kg), full(wkv)],
        out_specs=[hblk(QK_HEAD), hblk(QK_HEAD), hblk(2 * V_DIM)],
        out_shape=[jax.ShapeDtypeStruct((batch, MLA_HEADS, seq, QK_HEAD), BF16),
                   jax.ShapeDtypeStruct((batch, MLA_HEADS, seq, QK_HEAD), BF16),
                   jax.ShapeDtypeStruct((batch, MLA_HEADS, seq, 2 * V_DIM), BF16)],
        compiler_params=_cparams(("parallel",)),
        name="mla_prep",
    )(u, u, u, cs, qg, wq_p, kg, wkv)


def _attn_kernel(q_ref, k_ref, v_ref, g_ref, o_ref, m_ref, acc_ref, s0, s1, p0, p1, a0, a1, x0, x1, *, tq, sub):
    tk = tq // 2
    i = pl.program_id(2)
    s_scr, p_scr, a_scr, x_scr = (s0, s1), (p0, p1), (a0, a1), (x0, x1)
    lo, hi, full = slice(0, tk), slice(tk, tq), slice(0, tq)
    nt = (((1,), (1,)), ((), ()))

    m_ref[...] = jnp.full(m_ref.shape, -jnp.inf, F32)
    acc_ref[...] = jnp.zeros(acc_ref.shape, F32)
    p1[...] = jnp.zeros(p1.shape, BF16)
    a1[...] = jnp.ones(a1.shape, F32)

    def qk(t, slot, rows):
        kb = k_ref[pl.ds(pl.multiple_of(t * tk, tk), tk), :]
        for r0 in range(rows.start, rows.stop, sub):
            rs = slice(r0, r0 + sub)
            s = lax.dot_general(q_ref[rs, :], kb, nt, preferred_element_type=F32)
            s_scr[slot][rs, :] = s
            x_scr[slot][rs, :] = jnp.broadcast_to(jnp.max(s, axis=1, keepdims=True), (sub, LANES))

    def softmax(slot, rows, causal):
        for r0 in range(rows.start, rows.stop, sub):
            rs = slice(r0, r0 + sub)
            s = s_scr[slot][rs, :]
            if causal:
                r = lax.broadcasted_iota(jnp.int32, s.shape, 0) + (r0 - rows.start)
                c = lax.broadcasted_iota(jnp.int32, s.shape, 1)
                s = jnp.where(c <= r, s, -jnp.inf)
                s_max = jnp.max(s, axis=1, keepdims=True)
            else:
                s_max = x_scr[slot][rs, :]
            m_prev = m_ref[rs, :]
            m_new = jnp.maximum(m_prev, s_max)
            p_scr[slot][rs, :] = jnp.exp2(s - pltpu.repeat(m_new, tk // LANES, 1)).astype(BF16)
            a_scr[slot][rs, :] = jnp.exp2(m_prev - m_new)
            m_ref[rs, :] = m_new

    def pv(t, slot, rows):
        vb = v_ref[pl.ds(pl.multiple_of(t * tk, tk), tk), :]
        acc_ref[rows, :] = (pltpu.repeat(a_scr[slot][rows, :], 2, 1) * acc_ref[rows, :]
                            + jnp.dot(p_scr[slot][rows, :], vb, preferred_element_type=F32))

    def step(t, slot):
        qk(t + 1, 1 - slot, full)
        pv(jnp.maximum(t - 1, 0), 1 - slot, full)
        softmax(slot, full, False)

    def pair(tt):
        step(2 * tt, 0)
        step(2 * tt + 1, 1)

    qk(0, 0, full)
    one, two = i & 1, i & 2

    @pl.when(one == 1)
    def _():
        pair(0)

    @pl.when(two == 2)
    def _():
        pair(one)
        pair(one + 1)

    def four_pairs(n, carry):
        for j in range(4):
            pair(one + two + 4 * n + j)
        return carry

    lax.fori_loop(0, i // 4, four_pairs, 0)
    d = 2 * i
    qk(d + 1, 1, hi)
    pv(jnp.maximum(d - 1, 0), 1, full)
    softmax(0, lo, True)
    softmax(0, hi, False)
    softmax(1, hi, True)
    pv(d, 0, full)
    pv(d + 1, 1, hi)
    g = g_ref[...].astype(F32)
    acc = acc_ref[...]
    o_ref[...] = (acc[:, :V_DIM] / acc[:, V_DIM:] * (g * jax.nn.sigmoid(g))).astype(o_ref.dtype)


def _attention(q, k, v_aug, u, seq, tq, sub):
    batch = q.shape[0]
    t = batch * seq
    nq = seq // tq
    return pl.pallas_call(
        functools.partial(_attn_kernel, tq=tq, sub=sub),
        grid=(batch, MLA_HEADS, nq),
        in_specs=[pl.BlockSpec((None, None, tq, QK_HEAD), lambda b, h, i: (b, h, i, 0)),
                  pl.BlockSpec((None, None, seq, QK_HEAD), lambda b, h, i: (b, h, 0, 0)),
                  pl.BlockSpec((None, None, seq, 2 * V_DIM), lambda b, h, i: (b, h, 0, 0)),
                  pl.BlockSpec((tq, V_DIM), lambda b, h, i: (b * nq + i, U_MG // V_DIM + h))],
        out_specs=pl.BlockSpec((tq, V_DIM), lambda b, h, i: (b * nq + i, h)),
        out_shape=jax.ShapeDtypeStruct((t, MLA_DIM), BF16),
        scratch_shapes=[pltpu.VMEM((tq, LANES), F32), pltpu.VMEM((tq, 2 * V_DIM), F32),
                        pltpu.VMEM((tq, tq // 2), F32), pltpu.VMEM((tq, tq // 2), F32),
                        pltpu.VMEM((tq, tq // 2), BF16), pltpu.VMEM((tq, tq // 2), BF16),
                        pltpu.VMEM((tq, LANES), F32), pltpu.VMEM((tq, LANES), F32),
                        pltpu.VMEM((tq, LANES), F32), pltpu.VMEM((tq, LANES), F32)],
        compiler_params=_cparams(("parallel", "parallel", "arbitrary")),
        name="mla_attention",
    )(q, k, v_aug, u)


DECAY_SCALE = 0.6065306597126334


def _rwkv_prep_kernel(r_ref, k_ref, v_ref, wa_ref, rp_ref, kp_ref, vp_ref, wap_ref,
                      mu_r, mu_k, mu_v, mu_wa, w0, w2p, a0, a2p, kkw, kaw, eones,
                      r_o, lw_o, k_o, v_o, a_o, b_o, *, tiles_per_seq):
    first = (pl.program_id(0) % tiles_per_seq) == 0
    row = lax.broadcasted_iota(jnp.int32, r_ref.shape, 0)
    row_wa = lax.broadcasted_iota(jnp.int32, wa_ref.shape, 0)

    def lerp(x_ref, p_ref, mu, rw):
        x = x_ref[...].astype(F32)
        prev = jnp.where(first, 0.0, p_ref[...].astype(F32))
        return x + (_shift_rows(x, prev, 1, rw) - x) * mu[...]

    r = lerp(r_ref, rp_ref, mu_r, row)
    k = lerp(k_ref, kp_ref, mu_k, row)
    v = lerp(v_ref, vp_ref, mu_v, row)
    wa = lerp(wa_ref, wap_ref, mu_wa, row_wa)
    z = w0[...] + _bdot(jnp.tanh(wa), w2p[...])
    a_sig = jax.nn.sigmoid(a0[...] + _bdot(wa, a2p[...]))
    kk = k * kkw[...]
    ss = _bdot(kk * kk, eones[...])
    kk = kk * jnp.minimum(lax.rsqrt(ss), 1e12)
    r_o[...] = r
    lw_o[...] = -DECAY_SCALE * jax.nn.sigmoid(z)
    k_o[...] = k * (1.0 + (a_sig - 1.0) * kaw[...])
    v_o[...] = v
    a_o[...] = -kk
    b_o[...] = kk * a_sig


def _rwkv_prep(u, mu_r, mu_k, mu_v, mu_wa, w0, w2p, a0, a2p, kkw, kaw, eones, seq, tm):
    t = u.shape[0]
    cblk = lambda c: pl.BlockSpec((tm, RWKV_DIM), lambda i, c=c: (i, c))
    prow = lambda i: jnp.maximum(i * (tm // PREV_ROWS) - 1, 0)
    pblk = lambda c: pl.BlockSpec((PREV_ROWS, RWKV_DIM), lambda i, c=c: (prow(i), c))
    full = lambda a: pl.BlockSpec(a.shape, lambda i: (0,) * a.ndim)
    out = jax.ShapeDtypeStruct((t, RWKV_DIM), F32)
    oblk = pl.BlockSpec((tm, RWKV_DIM), lambda i: (i, 0))
    return pl.pallas_call(
        functools.partial(_rwkv_prep_kernel, tiles_per_seq=seq // tm),
        grid=(t // tm,),
        in_specs=[cblk(U_R // 512), cblk(U_K // 512), cblk(U_V // 512),
                  pl.BlockSpec((tm, LANES), lambda i: (i, U_WA // LANES)),
                  pblk(U_R // 512), pblk(U_K // 512), pblk(U_V // 512),
                  pl.BlockSpec((PREV_ROWS, LANES), lambda i: (prow(i), U_WA // LANES)),
                  full(mu_r), full(mu_k), full(mu_v), full(mu_wa), full(w0), full(w2p),
                  full(a0), full(a2p), full(kkw), full(kaw), full(eones)],
        out_specs=[oblk] * 6,
        out_shape=[out] * 6,
        compiler_params=_cparams(("parallel",)),
        name="rwkv_prep",
    )(u, u, u, u, u, u, u, u, mu_r, mu_k, mu_v, mu_wa, w0, w2p, a0, a2p, kkw, kaw, eones)


def _stack(x, even_lane):
    return jnp.concatenate([jnp.where(even_lane, x, 0.0), jnp.where(even_lane, 0.0, x)], axis=0)


def _unstack(x):
    return x[:CHUNK] + x[CHUNK:]


def _each(fn, *lists):
    return [fn(*args) for args in zip(*lists)]


def _chunk_terms(r, lw, k, v, a, b, masks):
    even_lane, strict, incl, same_blk, eye, tri = masks
    nt = (((1,), (1,)), ((), ()))
    tn = (((0,), (0,)), ((), ()))

    def cumsum(x):
        hi = x.astype(BF16)
        lo = (x - hi.astype(F32)).astype(BF16)
        return (jnp.dot(tri, hi, preferred_element_type=F32)
                + jnp.dot(tri, lo, preferred_element_type=F32))

    cum = _each(cumsum, lw)
    cum_end = _each(lambda c: c[CHUNK - 1:CHUNK], cum)
    at_s = _each(lambda x, c, w: _stack(x * jnp.exp(c - w), even_lane), a, cum, lw)
    rt_s = _each(lambda x, c: _stack(x * jnp.exp(c), even_lane), r, cum)
    v_s = _each(lambda x: _stack(x, even_lane), v)
    bt = _each(lambda x, c: (x * jnp.exp(-c)).astype(BF16), b, cum)
    kt = _each(lambda x, c: (x * jnp.exp(-c)).astype(BF16), k, cum)
    g = _each(lambda x, y, p, q: lax.dot_general(
        jnp.concatenate([x, y], axis=0).astype(BF16), jnp.concatenate([p, p, q, q], axis=0), nt,
        preferred_element_type=F32), at_s, rt_s, bt, kt)
    x = _each(lambda m: jnp.where(strict, m[:LANES, :LANES], 0.0), g)
    a_ak = _each(lambda m: jnp.where(strict, m[:LANES, LANES:], 0.0), g)
    a_r = _each(lambda m: jnp.concatenate([jnp.where(incl, m[LANES:, :LANES], 0.0),
                                           jnp.where(incl, m[LANES:, LANES:], 0.0)], axis=1), g)
    akv = _each(_bdot, a_ak, v_s)

    tinv = _each(lambda m: jnp.where(eye, 1.0, 0.0) + m, x)
    n = 2
    while n < CHUNK:
        x = _each(lambda m: _bdot(m, m), x)
        tinv = _each(lambda t, m: t + _bdot(t, m), tinv, x)
        n *= 2

    wu = _each(lambda t, p, q: _bdot(t, jnp.concatenate([p, q], axis=1)), tinv, at_s, akv)
    zeros = jnp.zeros((LANES, LANES), F32)
    top = _each(lambda m, w, q: _bdot(m, jnp.concatenate(
        [w, jnp.concatenate([zeros, q], axis=1)], axis=0)), a_r, wu, v_s)
    rw_s = _each(lambda x, t: x + t[:, :LANES], rt_s, top)
    y0_s = _each(lambda t: t[:, LANES:], top)

    def carry_terms(bb, kk, vv, c, ce, w):
        to_end = jnp.exp(ce - c)
        bk = jnp.concatenate([bb * to_end, kk * to_end], axis=0)
        rhs = jnp.concatenate(
            [jnp.concatenate([_unstack(w[:, :LANES]), _unstack(w[:, LANES:])], axis=1),
             jnp.concatenate([jnp.zeros((CHUNK, LANES), F32), vv], axis=1)], axis=0)
        mz = lax.dot_general(bk.astype(BF16), rhs.astype(BF16), tn, preferred_element_type=F32)
        m = jnp.where(same_blk, mz[:, :LANES], 0.0) + jnp.where(eye, jnp.exp(ce), 0.0)
        return m, jnp.where(same_blk, mz[:, LANES:], 0.0)

    mz = _each(carry_terms, b, k, v, cum, cum_end, wu)
    return rw_s, y0_s, [t[0] for t in mz], [t[1] for t in mz]


def _rwkv_scan_kernel(r_ref, lw_ref, k_ref, v_ref, a_ref, b_ref, g_ref, gng, gnb, rkw, eones,
                      o_ref, state_ref, y_ref, *, tt):
    @pl.when(pl.program_id(0) == 0)
    def _():
        state_ref[...] = jnp.zeros(state_ref.shape, F32)

    rr = lax.broadcasted_iota(jnp.int32, (LANES, LANES), 0)
    cc = lax.broadcasted_iota(jnp.int32, (LANES, LANES), 1)
    same_blk = (rr // CHUNK) == (cc // CHUNK)
    strict = same_blk & ((cc % CHUNK) < (rr % CHUNK))
    incl = same_blk & ((cc % CHUNK) <= (rr % CHUNK))
    eye = rr == cc
    even_lane = lax.broadcasted_iota(jnp.int32, (CHUNK, LANES), 1) < CHUNK
    tr = lax.broadcasted_iota(jnp.int32, (CHUNK, CHUNK), 0)
    tc = lax.broadcasted_iota(jnp.int32, (CHUNK, CHUNK), 1)
    tri = jnp.where(tc <= tr, 1.0, 0.0).astype(BF16)
    masks = (even_lane, strict, incl, same_blk, eye, tri)

    batch = r_ref.shape[0]
    chains = [(bi, slice(p * LANES, (p + 1) * LANES))
              for bi in range(batch) for p in range(RWKV_DIM // LANES)]

    nch = len(chains)

    def chunk_body(c, carry):
        rows = [pl.ds(pl.multiple_of((c * UNROLL + j) * CHUNK, CHUNK), CHUNK) for j in range(UNROLL)]
        load = lambda ref: [ref[bi, rw, cs] for rw in rows for bi, cs in chains]
        rw_s, y0_s, m, z0 = _chunk_terms(load(r_ref), load(lw_ref), load(k_ref), load(v_ref),
                                         load(a_ref), load(b_ref), masks)
        st = [state_ref[n] for n in range(nch)]
        for j in range(UNROLL):
            ys = [_bdot(jnp.concatenate([rw_s[j * nch + n], m[j * nch + n]], axis=0), st[n])
                  for n in range(nch)]
            for n, (bi, cs) in enumerate(chains):
                y_ref[bi, rows[j], cs] = _unstack(ys[n][:LANES] + y0_s[j * nch + n])
            st = [ys[n][LANES:] + z0[j * nch + n] for n in range(nch)]
        for n in range(nch):
            state_ref[n] = st[n]
        return carry

    lax.fori_loop(0, tt // (CHUNK * UNROLL), chunk_body, 0)

    e = eones[...]

    def headsum(x):
        return jnp.dot(x.astype(BF16), e, preferred_element_type=F32)

    for bi in range(batch):
        y = y_ref[bi]
        d = y - headsum(y) * (1.0 / RWKV_N)
        var = headsum(d * d) * (1.0 / RWKV_N)
        yn = d * lax.rsqrt(var + RWKV_GN_EPS) * gng[...] + gnb[...]
        yn = yn + headsum(r_ref[bi] * k_ref[bi] * rkw[...]) * v_ref[bi]
        g = g_ref[bi].astype(F32)
        o_ref[bi] = (yn * (g * jax.nn.sigmoid(g))).astype(o_ref.dtype)


def _rwkv_scan(r, lw, k, v, a, b, u, gng, gnb, rkw, eones, batch, seq, tt):
    to3 = lambda x: x.reshape(batch, seq, x.shape[-1])
    blk = pl.BlockSpec((batch, tt, RWKV_DIM), lambda i: (0, i, 0))
    full = lambda x: pl.BlockSpec(x.shape, lambda i: (0,) * x.ndim)
    out = pl.pallas_call(
        functools.partial(_rwkv_scan_kernel, tt=tt),
        grid=(seq // tt,),
        in_specs=[blk] * 6 + [pl.BlockSpec((batch, tt, RWKV_DIM), lambda i: (0, i, U_RG // RWKV_DIM)),
                              full(gng), full(gnb), full(rkw), full(eones)],
        out_specs=blk,
        out_shape=jax.ShapeDtypeStruct((batch, seq, RWKV_DIM), BF16),
        scratch_shapes=[pltpu.VMEM((batch * RWKV_DIM // LANES, LANES, LANES), F32),
                        pltpu.VMEM((batch, tt, RWKV_DIM), F32)],
        compiler_params=_cparams(("arbitrary",)),
        name="rwkv_scan",
    )(to3(r), to3(lw), to3(k), to3(v), to3(a), to3(b), to3(u), gng, gnb, rkw, eones)
    return out.reshape(batch * seq, RWKV_DIM)


def _outproj_kernel(cb, cc, ch, cg, ccp, chp, cw, ym, yr, x_ref, w1, w2, w3, lg, lb, o_ref, yc,
                    *, tiles_per_seq):
    first = (pl.program_id(0) % tiles_per_seq) == 0
    yc[...] = _conv_tile(cb, cc, ch, cg, ccp, chp, cw, first)
    sub = min(256, o_ref.shape[0])
    for r0 in range(0, o_ref.shape[0], sub):
        rs = slice(r0, r0 + sub)
        acc = jnp.dot(yc[rs, :], w1[...], preferred_element_type=F32)
        acc = acc + jnp.dot(ym[rs, :], w2[...], preferred_element_type=F32)
        acc = acc + jnp.dot(yr[rs, :], w3[...], preferred_element_type=F32)
        z = DEEPNORM_ALPHA * x_ref[rs, :] + acc
        mu = jnp.mean(z, -1, keepdims=True)
        d = z - mu
        var = jnp.mean(d * d, -1, keepdims=True)
        o_ref[rs, :] = d * lax.rsqrt(var + LN_EPS) * lg[...] + lb[...]


def _outproj(u, conv_w, ym, yr, x2, w1, w2, w3, lg, lb, seq, tm):
    t = x2.shape[0]
    rblk = lambda d: pl.BlockSpec((tm, d), lambda i: (i, 0))
    full = lambda a: pl.BlockSpec(a.shape, lambda i: (0,) * a.ndim)
    cblk = lambda c: pl.BlockSpec((tm, CONV_DIM), lambda i, c=c: (i, c))
    pblk = lambda c: pl.BlockSpec((PREV_ROWS, CONV_DIM),
                                  lambda i, c=c: (jnp.maximum(i * (tm // PREV_ROWS) - 1, 0), c))
    return pl.pallas_call(
        functools.partial(_outproj_kernel, tiles_per_seq=seq // tm),
        grid=(t // tm,),
        in_specs=[cblk(U_CB // 512), cblk(U_CC // 512), cblk(U_CH // 512), cblk(U_CG // 512),
                  pblk(U_CC // 512), pblk(U_CH // 512), full(conv_w),
                  rblk(MLA_DIM), rblk(RWKV_DIM), rblk(D_MODEL),
                  full(w1), full(w2), full(w3), full(lg), full(lb)],
        out_specs=rblk(D_MODEL),
        out_shape=jax.ShapeDtypeStruct((t, D_MODEL), F32),
        scratch_shapes=[pltpu.VMEM((tm, CONV_DIM), BF16)],
        compiler_params=_cparams(("parallel",)),
        name="outproj_ln",
    )(u, u, u, u, u, u, conv_w, ym, yr, x2, w1, w2, w3, lg, lb)


def _rot_half_cols(w):
    return jnp.concatenate([-w[:, QK_ROPE // 2:], w[:, :QK_ROPE // 2]], axis=1)


def _pack_w_in(w):
    o_ckv, o_kpe, o_mg, o_rc = 2560, 2816, 2880, 3904
    ckv = w[:, o_ckv:o_kpe]
    kpe = w[:, o_kpe:o_mg]
    mg = w[:, o_mg:o_rc]
    r = w[:, o_rc:o_rc + 512]
    wd = w[:, o_rc + 512:o_rc + 576]
    k = w[:, o_rc + 576:o_rc + 1088]
    v = w[:, o_rc + 1088:o_rc + 1600]
    ad = w[:, o_rc + 1600:o_rc + 1664]
    rg = w[:, o_rc + 1664:]
    return jnp.concatenate([w[:, :o_ckv], r, k, v, rg, mg, ckv, kpe, _rot_half_cols(kpe),
                            wd, ad], axis=1).astype(BF16)


def _pack_mu(mu):
    r, wd, k, v, ad = (mu[0:512], mu[512:576], mu[576:1088], mu[1088:1600], mu[1600:1664])
    row = lambda a: a.reshape(1, -1)
    return row(r), row(k), row(v), row(jnp.concatenate([wd, ad]))


def _pack_w_uq(w):
    cols = []
    for h in range(MLA_HEADS):
        wh = w[:, h * QK_HEAD:(h + 1) * QK_HEAD]
        pe = wh[:, QK_NOPE:]
        cols += [wh[:, :QK_NOPE], pe, _rot_half_cols(pe)]
    return jnp.concatenate(cols, axis=1).astype(BF16)


def _tile(n, pref):
    return pref if n % pref == 0 else n


def kernel(x, positions, w_in, conv_w, q_norm_g, w_uq, kv_norm_g, w_ukv, rwkv_mu, rwkv_w0,
           rwkv_w2, rwkv_a0, rwkv_a2, rwkv_k_k, rwkv_k_a, rwkv_r_k, rwkv_gn_g, rwkv_gn_b,
           w_out, ln_g, ln_b):
    batch, seq, _ = x.shape
    t = batch * seq
    row = lambda a: a.reshape(1, -1)

    inv_freq = ROPE_THETA ** (-jnp.arange(0, QK_ROPE, 2, dtype=F32) / QK_ROPE)
    invf4 = jnp.tile(inv_freq, 4).reshape(1, LANES)
    cs = _rope_table(positions.reshape(t, 1), invf4, _tile(t, 1024))

    head_of = jnp.arange(RWKV_DIM) // RWKV_N
    eones = (head_of[:, None] == head_of[None, :]).astype(BF16)
    zpad = jnp.zeros((DECAY_LORA, RWKV_DIM), F32)

    x2 = x.reshape(t, D_MODEL)
    for l in range(DEPTH):
        u = _inproj(x2, _pack_w_in(w_in[l]), _tile(t, 1024), 2048)
        q, k, v = _mla_prep(u, cs, row(q_norm_g[l]), _pack_w_uq(w_uq[l]), row(kv_norm_g[l]),
                            w_ukv[l].astype(BF16), batch, seq, _tile(seq, 1024))
        y_mla = _attention(q, k, v, u, seq, _tile(seq, 1024), 512)
        mu_r, mu_k, mu_v, mu_wa = _pack_mu(rwkv_mu[l])
        w2p = jnp.concatenate([rwkv_w2[l], zpad], axis=0).astype(BF16)
        a2p = jnp.concatenate([zpad, rwkv_a2[l]], axis=0).astype(BF16)
        rr, lw, kk, vv, aa, bb = _rwkv_prep(
            u, mu_r, mu_k, mu_v, mu_wa, row(rwkv_w0[l]), w2p, row(rwkv_a0[l]), a2p,
            row(rwkv_k_k[l]), row(rwkv_k_a[l]), eones, seq, _tile(seq, 1024))
        y_rwkv = _rwkv_scan(rr, lw, kk, vv, aa, bb, u, row(rwkv_gn_g[l]), row(rwkv_gn_b[l]),
                            row(rwkv_r_k[l]), eones, batch, seq, _tile(seq, 512))
        wo = w_out[l].astype(BF16)
        x2 = _outproj(u, conv_w[l], y_mla, y_rwkv, x2,
                      wo[:CONV_DIM], wo[CONV_DIM:CONV_DIM + MLA_DIM], wo[CONV_DIM + MLA_DIM:],
                      row(ln_g[l]), row(ln_b[l]), seq, _tile(seq, 512))
    return x2.reshape(batch, seq, D_MODEL)
```

```python
import functools

import jax
import jax.numpy as jnp
from jax import lax
from jax.experimental import pallas as pl
from jax.experimental.pallas import tpu as pltpu

F32 = jnp.float32
BF16 = jnp.bfloat16

D_MODEL = 2048
CONV_DIM = 512
CONV_K = 3
MLA_HEADS = 8
QK_NOPE = 128
QK_ROPE = 64
QK_HEAD = QK_NOPE + QK_ROPE
V_DIM = 128
MLA_DIM = MLA_HEADS * V_DIM
Q_LORA = 512
KV_LORA = 256
ROPE_THETA = 10000.0
RWKV_HEADS = 8
RWKV_N = 64
RWKV_DIM = RWKV_HEADS * RWKV_N
DECAY_LORA = 64
A_LORA = 64
RWKV_GN_EPS = 64e-5
LN_EPS = 1e-5
RMS_EPS = 1e-6
DEPTH = 2
DEEPNORM_ALPHA = (2 * DEPTH) ** 0.25

U_CB, U_CC, U_CH, U_CG = 0, 512, 1024, 1536
U_CQ = 2048
U_R, U_K, U_V, U_RG = 2560, 3072, 3584, 4096
U_MG = 4608
U_CKV = 5632
U_KPE = 5888
U_WA = 6016
U_TOTAL = 6144

LANES = 128
PREV_ROWS = 16
CHUNK = 64
UNROLL = 2
VMEM_LIMIT = 56 * 1024 * 1024

LOG2E = 1.4426950408889634


def _cparams(sem):
    return pltpu.CompilerParams(dimension_semantics=sem, vmem_limit_bytes=VMEM_LIMIT)


def _bdot(a, b):
    return jnp.dot(a.astype(BF16), b.astype(BF16), preferred_element_type=F32)


def _inproj_kernel(x_ref, w_ref, o_ref, xb_ref):
    @pl.when(pl.program_id(1) == 0)
    def _():
        xb_ref[...] = x_ref[...].astype(BF16)

    o_ref[...] = jnp.dot(xb_ref[...], w_ref[...], preferred_element_type=F32).astype(o_ref.dtype)


def _inproj(x2, w_p, tm, tn):
    t = x2.shape[0]
    return pl.pallas_call(
        _inproj_kernel,
        grid=(t // tm, U_TOTAL // tn),
        in_specs=[pl.BlockSpec((tm, D_MODEL), lambda i, j: (i, 0)),
                  pl.BlockSpec((D_MODEL, tn), lambda i, j: (0, j))],
        out_specs=pl.BlockSpec((tm, tn), lambda i, j: (i, j)),
        out_shape=jax.ShapeDtypeStruct((t, U_TOTAL), BF16),
        scratch_shapes=[pltpu.VMEM((tm, D_MODEL), BF16)],
        compiler_params=_cparams(("parallel", "arbitrary")),
        name="inproj",
    )(x2, w_p)


def _shift_rows(x, prev, n, row):
    out = pltpu.roll(x, n, 0)
    for r in range(n):
        out = jnp.where(row == r, prev[PREV_ROWS - n + r:PREV_ROWS - n + r + 1], out)
    return out


def _conv_tile(cb, cc, ch, cg, ccp, chp, w, first):
    up = cc[...].astype(F32) * ch[...].astype(F32)
    prev = jnp.where(first, 0.0, ccp[...].astype(F32) * chp[...].astype(F32))
    row = lax.broadcasted_iota(jnp.int32, up.shape, 0)
    u1 = _shift_rows(up, prev, 1, row)
    u2 = _shift_rows(up, prev, 2, row)
    wv = w[...]
    y = wv[0:1] * u2 + wv[1:2] * u1 + wv[2:3] * up
    g = cg[...].astype(F32)
    return (cb[...].astype(F32) * y * (g * jax.nn.sigmoid(g))).astype(BF16)


def _rope_kernel(pos_ref, invf_ref, cs_ref):
    ang = pos_ref[...].astype(F32) * invf_ref[...]
    lane = lax.broadcasted_iota(jnp.int32, ang.shape, 1)
    cs_ref[...] = jnp.where(lane < QK_ROPE, jnp.cos(ang), jnp.sin(ang))


def _rope_table(pos2, invf4, tm):
    t = pos2.shape[0]
    return pl.pallas_call(
        _rope_kernel,
        grid=(t // tm,),
        in_specs=[pl.BlockSpec((tm, 1), lambda i: (i, 0)),
                  pl.BlockSpec((1, LANES), lambda i: (0, 0))],
        out_specs=pl.BlockSpec((tm, LANES), lambda i: (i, 0)),
        out_shape=jax.ShapeDtypeStruct((t, LANES), F32),
        compiler_params=_cparams(("parallel",)),
        name="rope_table",
    )(pos2, invf4)


def _rms(x, g):
    return x * lax.rsqrt(jnp.mean(x * x, -1, keepdims=True) + RMS_EPS) * g


def _mla_prep_kernel(cq_ref, ckv_ref, kp_ref, cs_ref, qg_ref, wq_ref, kg_ref, wkv_ref,
                     q_out, k_out, v_out, *, qscale):
    cs = cs_ref[...]
    qf = _bdot(_rms(cq_ref[...].astype(F32), qg_ref[...]), wq_ref[...])
    kvf = _bdot(_rms(ckv_ref[...].astype(F32), kg_ref[...]), wkv_ref[...])
    kp = kp_ref[...].astype(F32) * cs
    kpe = (kp + pltpu.roll(kp, QK_ROPE, 1))[:, :QK_ROPE].astype(BF16)
    for h in range(MLA_HEADS):
        c0 = 2 * LANES * h
        pp = qf[:, c0 + LANES:c0 + 2 * LANES] * cs
        pe = pp + pltpu.roll(pp, QK_ROPE, 1)
        q_out[h, :, 0:QK_NOPE] = (qf[:, c0:c0 + LANES] * qscale).astype(BF16)
        q_out[h, :, QK_NOPE:QK_HEAD] = (pe[:, :QK_ROPE] * qscale).astype(BF16)
        k_out[h, :, 0:QK_NOPE] = kvf[:, c0:c0 + LANES].astype(BF16)
        k_out[h, :, QK_NOPE:QK_HEAD] = kpe
        v_out[h, :, 0:V_DIM] = kvf[:, c0 + LANES:c0 + 2 * LANES].astype(BF16)
        v_out[h, :, V_DIM:2 * V_DIM] = jnp.ones((kvf.shape[0], V_DIM), BF16)


def _mla_prep(u, cs, qg, wq_p, kg, wkv, batch, seq, tm):
    t = u.shape[0]
    nt = seq // tm
    hblk = lambda d: pl.BlockSpec((None, MLA_HEADS, tm, d), lambda i: (i // nt, 0, i % nt, 0))
    full = lambda a: pl.BlockSpec(a.shape, lambda i: (0,) * a.ndim)
    qscale = QK_HEAD ** -0.5 * LOG2E
    return pl.pallas_call(
        functools.partial(_mla_prep_kernel, qscale=qscale),
        grid=(t // tm,),
        in_specs=[pl.BlockSpec((tm, Q_LORA), lambda i: (i, U_CQ // Q_LORA)),
                  pl.BlockSpec((tm, KV_LORA), lambda i: (i, U_CKV // KV_LORA)),
                  pl.BlockSpec((tm, LANES), lambda i: (i, U_KPE // LANES)),
                  pl.BlockSpec((tm, LANES), lambda i: (i, 0)),
                  full(qg), full(wq_p), full(kg), full(wkv)],
        out_specs=[hblk(QK_HEAD), hblk(QK_HEAD), hblk(2 * V_DIM)],
        out_shape=[jax.ShapeDtypeStruct((batch, MLA_HEADS, seq, QK_HEAD), BF16),
                   jax.ShapeDtypeStruct((batch, MLA_HEADS, seq, QK_HEAD), BF16),
                   jax.ShapeDtypeStruct((batch, MLA_HEADS, seq, 2 * V_DIM), BF16)],
        compiler_params=_cparams(("parallel",)),
        name="mla_prep",
    )(u, u, u, cs, qg, wq_p, kg, wkv)


def _attn_kernel(q_ref, k_ref, v_ref, g_ref, o_ref, m_ref, acc_ref, s0, s1, p0, p1, a0, a1, x0, x1, *, tq, sub):
    tk = tq // 2
    i = pl.program_id(2)
    s_scr, p_scr, a_scr, x_scr = (s0, s1), (p0, p1), (a0, a1), (x0, x1)
    lo, hi, full = slice(0, tk), slice(tk, tq), slice(0, tq)
    nt = (((1,), (1,)), ((), ()))

    m_ref[...] = jnp.full(m_ref.shape, -jnp.inf, F32)
    acc_ref[...] = jnp.zeros(acc_ref.shape, F32)
    p1[...] = jnp.zeros(p1.shape, BF16)
    a1[...] = jnp.ones(a1.shape, F32)

    def qk(t, slot, rows):
        kb = k_ref[pl.ds(pl.multiple_of(t * tk, tk), tk), :]
        for r0 in range(rows.start, rows.stop, sub):
            rs = slice(r0, r0 + sub)
            s = lax.dot_general(q_ref[rs, :], kb, nt, preferred_element_type=F32)
            s_scr[slot][rs, :] = s
            x_scr[slot][rs, :] = jnp.broadcast_to(jnp.max(s, axis=1, keepdims=True), (sub, LANES))

    def softmax(slot, rows, causal):
        for r0 in range(rows.start, rows.stop, sub):
            rs = slice(r0, r0 + sub)
            s = s_scr[slot][rs, :]
            if causal:
                r = lax.broadcasted_iota(jnp.int32, s.shape, 0) + (r0 - rows.start)
                c = lax.broadcasted_iota(jnp.int32, s.shape, 1)
                s = jnp.where(c <= r, s, -jnp.inf)
                s_max = jnp.max(s, axis=1, keepdims=True)
            else:
                s_max = x_scr[slot][rs, :]
            m_prev = m_ref[rs, :]
            m_new = jnp.maximum(m_prev, s_max)
            p_scr[slot][rs, :] = jnp.exp2(s - jnp.tile(m_new, (1, tk // LANES))).astype(BF16)
            a_scr[slot][rs, :] = jnp.exp2(m_prev - m_new)
            m_ref[rs, :] = m_new

    def pv(t, slot, rows):
        vb = v_ref[pl.ds(pl.multiple_of(t * tk, tk), tk), :]
        acc_ref[rows, :] = (jnp.tile(a_scr[slot][rows, :], (1, 2)) * acc_ref[rows, :]
                            + jnp.dot(p_scr[slot][rows, :], vb, preferred_element_type=F32))

    def step(t, slot):
        qk(t + 1, 1 - slot, full)
        pv(jnp.maximum(t - 1, 0), 1 - slot, full)
        softmax(slot, full, False)

    def pair(tt):
        step(2 * tt, 0)
        step(2 * tt + 1, 1)

    qk(0, 0, full)
    one, two = i & 1, i & 2

    @pl.when(one == 1)
    def _():
        pair(0)

    @pl.when(two == 2)
    def _():
        pair(one)
        pair(one + 1)

    def four_pairs(n, carry):
        for j in range(4):
            pair(one + two + 4 * n + j)
        return carry

    lax.fori_loop(0, i // 4, four_pairs, 0)
    d = 2 * i
    qk(d + 1, 1, hi)
    pv(jnp.maximum(d - 1, 0), 1, full)
    softmax(0, lo, True)
    softmax(0, hi, False)
    softmax(1, hi, True)
    pv(d, 0, full)
    pv(d + 1, 1, hi)
    g = g_ref[...].astype(F32)
    acc = acc_ref[...]
    o_ref[...] = (acc[:, :V_DIM] / acc[:, V_DIM:] * (g * jax.nn.sigmoid(g))).astype(o_ref.dtype)


def _attention(q, k, v_aug, u, seq, tq, sub):
    batch = q.shape[0]
    t = batch * seq
    nq = seq // tq
    return pl.pallas_call(
        functools.partial(_attn_kernel, tq=tq, sub=sub),
        grid=(batch, MLA_HEADS, nq),
        in_specs=[pl.BlockSpec((None, None, tq, QK_HEAD), lambda b, h, i: (b, h, i, 0)),
                  pl.BlockSpec((None, None, seq, QK_HEAD), lambda b, h, i: (b, h, 0, 0)),
                  pl.BlockSpec((None, None, seq, 2 * V_DIM), lambda b, h, i: (b, h, 0, 0)),
                  pl.BlockSpec((tq, V_DIM), lambda b, h, i: (b * nq + i, U_MG // V_DIM + h))],
        out_specs=pl.BlockSpec((tq, V_DIM), lambda b, h, i: (b * nq + i, h)),
        out_shape=jax.ShapeDtypeStruct((t, MLA_DIM), BF16),
        scratch_shapes=[pltpu.VMEM((tq, LANES), F32), pltpu.VMEM((tq, 2 * V_DIM), F32),
                        pltpu.VMEM((tq, tq // 2), F32), pltpu.VMEM((tq, tq // 2), F32),
                        pltpu.VMEM((tq, tq // 2), BF16), pltpu.VMEM((tq, tq // 2), BF16),
                        pltpu.VMEM((tq, LANES), F32), pltpu.VMEM((tq, LANES), F32),
                        pltpu.VMEM((tq, LANES), F32), pltpu.VMEM((tq, LANES), F32)],
        compiler_params=_cparams(("parallel", "parallel", "arbitrary")),
        name="mla_attention",
    )(q, k, v_aug, u)


DECAY_SCALE = 0.6065306597126334


def _rwkv_prep_kernel(r_ref, k_ref, v_ref, wa_ref, rp_ref, kp_ref, vp_ref, wap_ref,
                      mu_r, mu_k, mu_v, mu_wa, w0, w2p, a0, a2p, kkw, kaw, eones,
                      r_o, lw_o, k_o, v_o, a_o, b_o, *, tiles_per_seq):
    first = (pl.program_id(0) % tiles_per_seq) == 0
    row = lax.broadcasted_iota(jnp.int32, r_ref.shape, 0)
    row_wa = lax.broadcasted_iota(jnp.int32, wa_ref.shape, 0)

    def lerp(x_ref, p_ref, mu, rw):
        x = x_ref[...].astype(F32)
        prev = jnp.where(first, 0.0, p_ref[...].astype(F32))
        return x + (_shift_rows(x, prev, 1, rw) - x) * mu[...]

    r = lerp(r_ref, rp_ref, mu_r, row)
    k = lerp(k_ref, kp_ref, mu_k, row)
    v = lerp(v_ref, vp_ref, mu_v, row)
    wa = lerp(wa_ref, wap_ref, mu_wa, row_wa)
    z = w0[...] + _bdot(jnp.tanh(wa), w2p[...])
    a_sig = jax.nn.sigmoid(a0[...] + _bdot(wa, a2p[...]))
    kk = k * kkw[...]
    ss = _bdot(kk * kk, eones[...])
    kk = kk * jnp.minimum(lax.rsqrt(ss), 1e12)
    r_o[...] = r
    lw_o[...] = -DECAY_SCALE * jax.nn.sigmoid(z)
    k_o[...] = k * (1.0 + (a_sig - 1.0) * kaw[...])
    v_o[...] = v
    a_o[...] = -kk
    b_o[...] = kk * a_sig


def _rwkv_prep(u, mu_r, mu_k, mu_v, mu_wa, w0, w2p, a0, a2p, kkw, kaw, eones, seq, tm):
    t = u.shape[0]
    cblk = lambda c: pl.BlockSpec((tm, RWKV_DIM), lambda i, c=c: (i, c))
    prow = lambda i: jnp.maximum(i * (tm // PREV_ROWS) - 1, 0)
    pblk = lambda c: pl.BlockSpec((PREV_ROWS, RWKV_DIM), lambda i, c=c: (prow(i), c))
    full = lambda a: pl.BlockSpec(a.shape, lambda i: (0,) * a.ndim)
    out = jax.ShapeDtypeStruct((t, RWKV_DIM), F32)
    oblk = pl.BlockSpec((tm, RWKV_DIM), lambda i: (i, 0))
    return pl.pallas_call(
        functools.partial(_rwkv_prep_kernel, tiles_per_seq=seq // tm),
        grid=(t // tm,),
        in_specs=[cblk(U_R // 512), cblk(U_K // 512), cblk(U_V // 512),
                  pl.BlockSpec((tm, LANES), lambda i: (i, U_WA // LANES)),
                  pblk(U_R // 512), pblk(U_K // 512), pblk(U_V // 512),
                  pl.BlockSpec((PREV_ROWS, LANES), lambda i: (prow(i), U_WA // LANES)),
                  full(mu_r), full(mu_k), full(mu_v), full(mu_wa), full(w0), full(w2p),
                  full(a0), full(a2p), full(kkw), full(kaw), full(eones)],
        out_specs=[oblk] * 6,
        out_shape=[out] * 6,
        compiler_params=_cparams(("parallel",)),
        name="rwkv_prep",
    )(u, u, u, u, u, u, u, u, mu_r, mu_k, mu_v, mu_wa, w0, w2p, a0, a2p, kkw, kaw, eones)


def _stack(x, even_lane):
    return jnp.concatenate([jnp.where(even_lane, x, 0.0), jnp.where(even_lane, 0.0, x)], axis=0)


def _unstack(x):
    return x[:CHUNK] + x[CHUNK:]


def _each(fn, *lists):
    return [fn(*args) for args in zip(*lists)]


def _chunk_terms(r, lw, k, v, a, b, masks):
    even_lane, strict, incl, same_blk, eye, tri = masks
    nt = (((1,), (1,)), ((), ()))
    tn = (((0,), (0,)), ((), ()))

    def cumsum(x):
        hi = x.astype(BF16)
        lo = (x - hi.astype(F32)).astype(BF16)
        return jnp.dot(tri, jnp.concatenate([hi, lo], axis=0), preferred_element_type=F32)

    cum = _each(cumsum, lw)
    cum_end = _each(lambda c: c[CHUNK - 1:CHUNK], cum)
    at_s = _each(lambda x, c, w: _stack(x * jnp.exp(c - w), even_lane), a, cum, lw)
    rt_s = _each(lambda x, c: _stack(x * jnp.exp(c), even_lane), r, cum)
    v_s = _each(lambda x: _stack(x, even_lane), v)
    bt = _each(lambda x, c: (x * jnp.exp(-c)).astype(BF16), b, cum)
    kt = _each(lambda x, c: (x * jnp.exp(-c)).astype(BF16), k, cum)
    g = _each(lambda x, y, p, q: lax.dot_general(
        jnp.concatenate([x, y], axis=0).astype(BF16), jnp.concatenate([p, p, q, q], axis=0), nt,
        preferred_element_type=F32), at_s, rt_s, bt, kt)
    x = _each(lambda m: jnp.where(strict, m[:LANES, :LANES], 0.0), g)
    a_ak = _each(lambda m: jnp.where(strict, m[:LANES, LANES:], 0.0), g)
    a_r = _each(lambda m: jnp.concatenate([jnp.where(incl, m[LANES:, :LANES], 0.0),
                                           jnp.where(incl, m[LANES:, LANES:], 0.0)], axis=1), g)
    akv = _each(_bdot, a_ak, v_s)

    tinv = _each(lambda m: jnp.where(eye, 1.0, 0.0) + m, x)
    n = 2
    while n < CHUNK:
        x = _each(lambda m: _bdot(m, m), x)
        tinv = _each(lambda t, m: t + _bdot(t, m), tinv, x)
        n *= 2

    wu = _each(lambda t, p, q: _bdot(t, jnp.concatenate([p, q], axis=1)), tinv, at_s, akv)
    zeros = jnp.zeros((LANES, LANES), F32)
    top = _each(lambda m, w, q: _bdot(m, jnp.concatenate(
        [w, jnp.concatenate([zeros, q], axis=1)], axis=0)), a_r, wu, v_s)
    rw_s = _each(lambda x, t: x + t[:, :LANES], rt_s, top)
    y0_s = _each(lambda t: t[:, LANES:], top)

    def carry_terms(bb, kk, vv, c, ce, w):
        to_end = jnp.exp(ce - c)
        bk = jnp.concatenate([bb * to_end, kk * to_end], axis=0)
        rhs = jnp.concatenate(
            [jnp.concatenate([_unstack(w[:, :LANES]), _unstack(w[:, LANES:])], axis=1),
             jnp.concatenate([jnp.zeros((CHUNK, LANES), F32), vv], axis=1)], axis=0)
        mz = lax.dot_general(bk.astype(BF16), rhs.astype(BF16), tn, preferred_element_type=F32)
        m = jnp.where(same_blk, mz[:, :LANES], 0.0) + jnp.where(eye, jnp.exp(ce), 0.0)
        return m, jnp.where(same_blk, mz[:, LANES:], 0.0)

    mz = _each(carry_terms, b, k, v, cum, cum_end, wu)
    return rw_s, y0_s, [t[0] for t in mz], [t[1] for t in mz]


def _rwkv_scan_kernel(r_ref, lw_ref, k_ref, v_ref, a_ref, b_ref, g_ref, gng, gnb, rkw, eones,
                      o_ref, state_ref, y_ref, *, tt):
    @pl.when(pl.program_id(0) == 0)
    def _():
        state_ref[...] = jnp.zeros(state_ref.shape, F32)

    rr = lax.broadcasted_iota(jnp.int32, (LANES, LANES), 0)
    cc = lax.broadcasted_iota(jnp.int32, (LANES, LANES), 1)
    same_blk = (rr // CHUNK) == (cc // CHUNK)
    strict = same_blk & ((cc % CHUNK) < (rr % CHUNK))
    incl = same_blk & ((cc % CHUNK) <= (rr % CHUNK))
    eye = rr == cc
    even_lane = lax.broadcasted_iota(jnp.int32, (CHUNK, LANES), 1) < CHUNK
    tr = lax.broadcasted_iota(jnp.int32, (CHUNK, CHUNK), 0)
    tc = lax.broadcasted_iota(jnp.int32, (CHUNK, CHUNK), 1)
    tri = jnp.where(tc <= tr, 1.0, 0.0).astype(BF16)
    tri = jnp.concatenate([tri, tri], axis=1)
    masks = (even_lane, strict, incl, same_blk, eye, tri)

    batch = r_ref.shape[0]
    chains = [(bi, slice(p * LANES, (p + 1) * LANES))
              for bi in range(batch) for p in range(RWKV_DIM // LANES)]

    nch = len(chains)

    def chunk_body(c, carry):
        rows = [pl.ds(pl.multiple_of((c * UNROLL + j) * CHUNK, CHUNK), CHUNK) for j in range(UNROLL)]
        load = lambda ref: [ref[bi, rw, cs] for rw in rows for bi, cs in chains]
        rw_s, y0_s, m, z0 = _chunk_terms(load(r_ref), load(lw_ref), load(k_ref), load(v_ref),
                                         load(a_ref), load(b_ref), masks)
        st = [state_ref[n] for n in range(nch)]
        for j in range(UNROLL):
            ys = [_bdot(jnp.concatenate([_unstack(rw_s[j * nch + n]), m[j * nch + n]], axis=0), st[n])
                  for n in range(nch)]
            for n, (bi, cs) in enumerate(chains):
                y_ref[bi, rows[j], cs] = ys[n][:CHUNK] + _unstack(y0_s[j * nch + n])
            st = [ys[n][CHUNK:] + z0[j * nch + n] for n in range(nch)]
        for n in range(nch):
            state_ref[n] = st[n]
        return carry

    lax.fori_loop(0, tt // (CHUNK * UNROLL), chunk_body, 0)

    e = eones[...]

    def headsum(x):
        return jnp.dot(x.astype(BF16), e, preferred_element_type=F32)

    for bi in range(batch):
        y = y_ref[bi]
        d = y - headsum(y) * (1.0 / RWKV_N)
        var = headsum(d * d) * (1.0 / RWKV_N)
        yn = d * lax.rsqrt(var + RWKV_GN_EPS) * gng[...] + gnb[...]
        yn = yn + headsum(r_ref[bi] * k_ref[bi] * rkw[...]) * v_ref[bi]
        g = g_ref[bi].astype(F32)
        o_ref[bi] = (yn * (g * jax.nn.sigmoid(g))).astype(o_ref.dtype)


def _rwkv_scan(r, lw, k, v, a, b, u, gng, gnb, rkw, eones, batch, seq, tt):
    to3 = lambda x: x.reshape(batch, seq, x.shape[-1])
    blk = pl.BlockSpec((batch, tt, RWKV_DIM), lambda i: (0, i, 0))
    full = lambda x: pl.BlockSpec(x.shape, lambda i: (0,) * x.ndim)
    out = pl.pallas_call(
        functools.partial(_rwkv_scan_kernel, tt=tt),
        grid=(seq // tt,),
        in_specs=[blk] * 6 + [pl.BlockSpec((batch, tt, RWKV_DIM), lambda i: (0, i, U_RG // RWKV_DIM)),
                              full(gng), full(gnb), full(rkw), full(eones)],
        out_specs=blk,
        out_shape=jax.ShapeDtypeStruct((batch, seq, RWKV_DIM), BF16),
        scratch_shapes=[pltpu.VMEM((batch * RWKV_DIM // LANES, LANES, LANES), F32),
                        pltpu.VMEM((batch, tt, RWKV_DIM), F32)],
        compiler_params=_cparams(("arbitrary",)),
        name="rwkv_scan",
    )(to3(r), to3(lw), to3(k), to3(v), to3(a), to3(b), to3(u), gng, gnb, rkw, eones)
    return out.reshape(batch * seq, RWKV_DIM)


def _outproj_kernel(cb, cc, ch, cg, ccp, chp, cw, ym, yr, x_ref, w1, w2, w3, lg, lb, o_ref, yc,
                    *, tiles_per_seq):
    first = (pl.program_id(0) % tiles_per_seq) == 0
    yc[...] = _conv_tile(cb, cc, ch, cg, ccp, chp, cw, first)
    sub = min(256, o_ref.shape[0])
    for r0 in range(0, o_ref.shape[0], sub):
        rs = slice(r0, r0 + sub)
        acc = jnp.dot(yc[rs, :], w1[...], preferred_element_type=F32)
        acc = acc + jnp.dot(ym[rs, :], w2[...], preferred_element_type=F32)
        acc = acc + jnp.dot(yr[rs, :], w3[...], preferred_element_type=F32)
        z = DEEPNORM_ALPHA * x_ref[rs, :] + acc
        mu = jnp.mean(z, -1, keepdims=True)
        d = z - mu
        var = jnp.mean(d * d, -1, keepdims=True)
        o_ref[rs, :] = d * lax.rsqrt(var + LN_EPS) * lg[...] + lb[...]


def _outproj(u, conv_w, ym, yr, x2, w1, w2, w3, lg, lb, seq, tm):
    t = x2.shape[0]
    rblk = lambda d: pl.BlockSpec((tm, d), lambda i: (i, 0))
    full = lambda a: pl.BlockSpec(a.shape, lambda i: (0,) * a.ndim)
    cblk = lambda c: pl.BlockSpec((tm, CONV_DIM), lambda i, c=c: (i, c))
    pblk = lambda c: pl.BlockSpec((PREV_ROWS, CONV_DIM),
                                  lambda i, c=c: (jnp.maximum(i * (tm // PREV_ROWS) - 1, 0), c))
    return pl.pallas_call(
        functools.partial(_outproj_kernel, tiles_per_seq=seq // tm),
        grid=(t // tm,),
        in_specs=[cblk(U_CB // 512), cblk(U_CC // 512), cblk(U_CH // 512), cblk(U_CG // 512),
                  pblk(U_CC // 512), pblk(U_CH // 512), full(conv_w),
                  rblk(MLA_DIM), rblk(RWKV_DIM), rblk(D_MODEL),
                  full(w1), full(w2), full(w3), full(lg), full(lb)],
        out_specs=rblk(D_MODEL),
        out_shape=jax.ShapeDtypeStruct((t, D_MODEL), F32),
        scratch_shapes=[pltpu.VMEM((tm, CONV_DIM), BF16)],
        compiler_params=_cparams(("parallel",)),
        name="outproj_ln",
    )(u, u, u, u, u, u, conv_w, ym, yr, x2, w1, w2, w3, lg, lb)


def _rot_half_cols(w):
    return jnp.concatenate([-w[:, QK_ROPE // 2:], w[:, :QK_ROPE // 2]], axis=1)


def _pack_w_in(w):
    o_ckv, o_kpe, o_mg, o_rc = 2560, 2816, 2880, 3904
    ckv = w[:, o_ckv:o_kpe]
    kpe = w[:, o_kpe:o_mg]
    mg = w[:, o_mg:o_rc]
    r = w[:, o_rc:o_rc + 512]
    wd = w[:, o_rc + 512:o_rc + 576]
    k = w[:, o_rc + 576:o_rc + 1088]
    v = w[:, o_rc + 1088:o_rc + 1600]
    ad = w[:, o_rc + 1600:o_rc + 1664]
    rg = w[:, o_rc + 1664:]
    return jnp.concatenate([w[:, :o_ckv], r, k, v, rg, mg, ckv, kpe, _rot_half_cols(kpe),
                            wd, ad], axis=1).astype(BF16)


def _pack_mu(mu):
    r, wd, k, v, ad = (mu[0:512], mu[512:576], mu[576:1088], mu[1088:1600], mu[1600:1664])
    row = lambda a: a.reshape(1, -1)
    return row(r), row(k), row(v), row(jnp.concatenate([wd, ad]))


def _pack_w_uq(w):
    cols = []
    for h in range(MLA_HEADS):
        wh = w[:, h * QK_HEAD:(h + 1) * QK_HEAD]
        pe = wh[:, QK_NOPE:]
        cols += [wh[:, :QK_NOPE], pe, _rot_half_cols(pe)]
    return jnp.concatenate(cols, axis=1).astype(BF16)


def _tile(n, pref):
    return pref if n % pref == 0 else n


def kernel(x, positions, w_in, conv_w, q_norm_g, w_uq, kv_norm_g, w_ukv, rwkv_mu, rwkv_w0,
           rwkv_w2, rwkv_a0, rwkv_a2, rwkv_k_k, rwkv_k_a, rwkv_r_k, rwkv_gn_g, rwkv_gn_b,
           w_out, ln_g, ln_b):
    batch, seq, _ = x.shape
    t = batch * seq
    row = lambda a: a.reshape(1, -1)

    inv_freq = ROPE_THETA ** (-jnp.arange(0, QK_ROPE, 2, dtype=F32) / QK_ROPE)
    invf4 = jnp.tile(inv_freq, 4).reshape(1, LANES)
    cs = _rope_table(positions.reshape(t, 1), invf4, _tile(t, 1024))

    head_of = jnp.arange(RWKV_DIM) // RWKV_N
    eones = (head_of[:, None] == head_of[None, :]).astype(BF16)
    zpad = jnp.zeros((DECAY_LORA, RWKV_DIM), F32)

    x2 = x.reshape(t, D_MODEL)
    for l in range(DEPTH):
        u = _inproj(x2, _pack_w_in(w_in[l]), _tile(t, 1024), 2048)
        q, k, v = _mla_prep(u, cs, row(q_norm_g[l]), _pack_w_uq(w_uq[l]), row(kv_norm_g[l]),
                            w_ukv[l].astype(BF16), batch, seq, _tile(seq, 1024))
        y_mla = _attention(q, k, v, u, seq, _tile(seq, 1024), 512)
        mu_r, mu_k, mu_v, mu_wa = _pack_mu(rwkv_mu[l])
        w2p = jnp.concatenate([rwkv_w2[l], zpad], axis=0).astype(BF16)
        a2p = jnp.concatenate([zpad, rwkv_a2[l]], axis=0).astype(BF16)
        rr, lw, kk, vv, aa, bb = _rwkv_prep(
            u, mu_r, mu_k, mu_v, mu_wa, row(rwkv_w0[l]), w2p, row(rwkv_a0[l]), a2p,
            row(rwkv_k_k[l]), row(rwkv_k_a[l]), eones, seq, _tile(seq, 1024))
        y_rwkv = _rwkv_scan(rr, lw, kk, vv, aa, bb, u, row(rwkv_gn_g[l]), row(rwkv_gn_b[l]),
                            row(rwkv_r_k[l]), eones, batch, seq, _tile(seq, 512))
        wo = w_out[l].astype(BF16)
        x2 = _outproj(u, conv_w[l], y_mla, y_rwkv, x2,
                      wo[:CONV_DIM], wo[CONV_DIM:CONV_DIM + MLA_DIM], wo[CONV_DIM + MLA_DIM:],
                      row(ln_g[l]), row(ln_b[l]), seq, _tile(seq, 512))
    return x2.reshape(batch, seq, D_MODEL)
```

```python
import functools

import jax
import jax.numpy as jnp
from jax import lax
from jax.experimental import pallas as pl
from jax.experimental.pallas import tpu as pltpu

F32 = jnp.float32
BF16 = jnp.bfloat16

D_MODEL = 2048
CONV_DIM = 512
CONV_K = 3
MLA_HEADS = 8
QK_NOPE = 128
QK_ROPE = 64
QK_HEAD = QK_NOPE + QK_ROPE
V_DIM = 128
MLA_DIM = MLA_HEADS * V_DIM
Q_LORA = 512
KV_LORA = 256
ROPE_THETA = 10000.0
RWKV_HEADS = 8
RWKV_N = 64
RWKV_DIM = RWKV_HEADS * RWKV_N
DECAY_LORA = 64
A_LORA = 64
RWKV_GN_EPS = 64e-5
LN_EPS = 1e-5
RMS_EPS = 1e-6
DEPTH = 2
DEEPNORM_ALPHA = (2 * DEPTH) ** 0.25

U_CB, U_CC, U_CH, U_CG = 0, 512, 1024, 1536
U_CQ = 2048
U_R, U_K, U_V, U_RG = 2560, 3072, 3584, 4096
U_MG = 4608
U_CKV = 5632
U_KPE = 5888
U_WA = 6016
U_TOTAL = 6144

LANES = 128
PREV_ROWS = 16
CHUNK = 64
UNROLL = 2
VMEM_LIMIT = 56 * 1024 * 1024

TM_PROJ = 1024
TN_PROJ = 2048
TM_PREP = 1024
TQ_ATTN = 1024
SUB_ATTN = 512
TT_SCAN = 512
TM_OUT = 512

LOG2E = 1.4426950408889634


def _cparams(sem):
    return pltpu.CompilerParams(dimension_semantics=sem, vmem_limit_bytes=VMEM_LIMIT)


def _bdot(a, b):
    return jnp.dot(a.astype(BF16), b.astype(BF16), preferred_element_type=F32)


def _inproj_kernel(x_ref, w_ref, o_ref, xb_ref):
    @pl.when(pl.program_id(1) == 0)
    def _():
        xb_ref[...] = x_ref[...].astype(BF16)

    o_ref[...] = jnp.dot(xb_ref[...], w_ref[...], preferred_element_type=F32).astype(o_ref.dtype)


def _inproj(x2, w_p, tm, tn):
    t = x2.shape[0]
    return pl.pallas_call(
        _inproj_kernel,
        grid=(t // tm, U_TOTAL // tn),
        in_specs=[pl.BlockSpec((tm, D_MODEL), lambda i, j: (i, 0)),
                  pl.BlockSpec((D_MODEL, tn), lambda i, j: (0, j))],
        out_specs=pl.BlockSpec((tm, tn), lambda i, j: (i, j)),
        out_shape=jax.ShapeDtypeStruct((t, U_TOTAL), BF16),
        scratch_shapes=[pltpu.VMEM((tm, D_MODEL), BF16)],
        compiler_params=_cparams(("parallel", "arbitrary")),
        name="inproj",
    )(x2, w_p)


def _shift_rows(x, prev, n, row):
    out = pltpu.roll(x, n, 0)
    for r in range(n):
        out = jnp.where(row == r, prev[PREV_ROWS - n + r:PREV_ROWS - n + r + 1], out)
    return out


def _conv_tile(cb, cc, ch, cg, ccp, chp, w, first):
    up = cc[...].astype(F32) * ch[...].astype(F32)
    prev = jnp.where(first, 0.0, ccp[...].astype(F32) * chp[...].astype(F32))
    row = lax.broadcasted_iota(jnp.int32, up.shape, 0)
    u1 = _shift_rows(up, prev, 1, row)
    u2 = _shift_rows(up, prev, 2, row)
    wv = w[...]
    y = wv[0:1] * u2 + wv[1:2] * u1 + wv[2:3] * up
    g = cg[...].astype(F32)
    return (cb[...].astype(F32) * y * (g * jax.nn.sigmoid(g))).astype(BF16)


def _rope_kernel(pos_ref, invf_ref, cs_ref):
    ang = pos_ref[...].astype(F32) * invf_ref[...]
    lane = lax.broadcasted_iota(jnp.int32, ang.shape, 1)
    cs_ref[...] = jnp.where(lane < QK_ROPE, jnp.cos(ang), jnp.sin(ang))


def _rope_table(pos2, invf4, tm):
    t = pos2.shape[0]
    return pl.pallas_call(
        _rope_kernel,
        grid=(t // tm,),
        in_specs=[pl.BlockSpec((tm, 1), lambda i: (i, 0)),
                  pl.BlockSpec((1, LANES), lambda i: (0, 0))],
        out_specs=pl.BlockSpec((tm, LANES), lambda i: (i, 0)),
        out_shape=jax.ShapeDtypeStruct((t, LANES), F32),
        compiler_params=_cparams(("parallel",)),
        name="rope_table",
    )(pos2, invf4)


def _rms(x, g):
    return x * lax.rsqrt(jnp.mean(x * x, -1, keepdims=True) + RMS_EPS) * g


def _mla_prep_kernel(cq_ref, ckv_ref, kp_ref, cs_ref, qg_ref, wq_ref, kg_ref, wkv_ref,
                     q_out, k_out, v_out, *, qscale):
    cs = cs_ref[...]
    qf = _bdot(_rms(cq_ref[...].astype(F32), qg_ref[...]), wq_ref[...])
    kvf = _bdot(_rms(ckv_ref[...].astype(F32), kg_ref[...]), wkv_ref[...])
    kp = kp_ref[...].astype(F32) * cs
    kpe = (kp + pltpu.roll(kp, QK_ROPE, 1))[:, :QK_ROPE].astype(BF16)
    for h in range(MLA_HEADS):
        c0 = 2 * LANES * h
        pp = qf[:, c0 + LANES:c0 + 2 * LANES] * cs
        pe = pp + pltpu.roll(pp, QK_ROPE, 1)
        q_out[h, :, 0:QK_NOPE] = (qf[:, c0:c0 + LANES] * qscale).astype(BF16)
        q_out[h, :, QK_NOPE:QK_HEAD] = (pe[:, :QK_ROPE] * qscale).astype(BF16)
        k_out[h, :, 0:QK_NOPE] = kvf[:, c0:c0 + LANES].astype(BF16)
        k_out[h, :, QK_NOPE:QK_HEAD] = kpe
        v_out[h, :, 0:V_DIM] = kvf[:, c0 + LANES:c0 + 2 * LANES].astype(BF16)
        v_out[h, :, V_DIM:2 * V_DIM] = jnp.ones((kvf.shape[0], V_DIM), BF16)


def _mla_prep(u, cs, qg, wq_p, kg, wkv, batch, seq, tm):
    t = u.shape[0]
    nt = seq // tm
    hblk = lambda d: pl.BlockSpec((None, MLA_HEADS, tm, d), lambda i: (i // nt, 0, i % nt, 0))
    full = lambda a: pl.BlockSpec(a.shape, lambda i: (0,) * a.ndim)
    qscale = QK_HEAD ** -0.5 * LOG2E
    return pl.pallas_call(
        functools.partial(_mla_prep_kernel, qscale=qscale),
        grid=(t // tm,),
        in_specs=[pl.BlockSpec((tm, Q_LORA), lambda i: (i, U_CQ // Q_LORA)),
                  pl.BlockSpec((tm, KV_LORA), lambda i: (i, U_CKV // KV_LORA)),
                  pl.BlockSpec((tm, LANES), lambda i: (i, U_KPE // LANES)),
                  pl.BlockSpec((tm, LANES), lambda i: (i, 0)),
                  full(qg), full(wq_p), full(kg), full(wkv)],
        out_specs=[hblk(QK_HEAD), hblk(QK_HEAD), hblk(2 * V_DIM)],
        out_shape=[jax.ShapeDtypeStruct((batch, MLA_HEADS, seq, QK_HEAD), BF16),
                   jax.ShapeDtypeStruct((batch, MLA_HEADS, seq, QK_HEAD), BF16),
                   jax.ShapeDtypeStruct((batch, MLA_HEADS, seq, 2 * V_DIM), BF16)],
        compiler_params=_cparams(("parallel",)),
        name="mla_prep",
    )(u, u, u, cs, qg, wq_p, kg, wkv)


def _attn_kernel(q_ref, k_ref, v_ref, g_ref, o_ref, m_ref, acc_ref, s0, s1, p0, p1, a0, a1, x0, x1, *, tq, sub):
    tk = tq // 2
    i = pl.program_id(2)
    s_scr, p_scr, a_scr, x_scr = (s0, s1), (p0, p1), (a0, a1), (x0, x1)
    lo, hi, full = slice(0, tk), slice(tk, tq), slice(0, tq)
    nt = (((1,), (1,)), ((), ()))

    m_ref[...] = jnp.full(m_ref.shape, -jnp.inf, F32)
    acc_ref[...] = jnp.zeros(acc_ref.shape, F32)
    p1[...] = jnp.zeros(p1.shape, BF16)
    a1[...] = jnp.ones(a1.shape, F32)

    def qk(t, slot, rows):
        kb = k_ref[pl.ds(pl.multiple_of(t * tk, tk), tk), :]
        for r0 in range(rows.start, rows.stop, sub):
            rs = slice(r0, r0 + sub)
            s = lax.dot_general(q_ref[rs, :], kb, nt, preferred_element_type=F32)
            s_scr[slot][rs, :] = s
            x_scr[slot][rs, :] = jnp.broadcast_to(jnp.max(s, axis=1, keepdims=True), (sub, LANES))

    def softmax(slot, rows, causal):
        for r0 in range(rows.start, rows.stop, sub):
            rs = slice(r0, r0 + sub)
            s = s_scr[slot][rs, :]
            if causal:
                r = lax.broadcasted_iota(jnp.int32, s.shape, 0) + (r0 - rows.start)
                c = lax.broadcasted_iota(jnp.int32, s.shape, 1)
                s = jnp.where(c <= r, s, -jnp.inf)
                s_max = jnp.max(s, axis=1, keepdims=True)
            else:
                s_max = x_scr[slot][rs, :]
            m_prev = m_ref[rs, :]
            m_new = jnp.maximum(m_prev, s_max)
            p_scr[slot][rs, :] = jnp.exp2(s - jnp.tile(m_new, (1, tk // LANES))).astype(BF16)
            a_scr[slot][rs, :] = jnp.exp2(m_prev - m_new)
            m_ref[rs, :] = m_new

    def pv(t, slot, rows):
        vb = v_ref[pl.ds(pl.multiple_of(t * tk, tk), tk), :]
        acc_ref[rows, :] = (jnp.tile(a_scr[slot][rows, :], (1, 2)) * acc_ref[rows, :]
                            + jnp.dot(p_scr[slot][rows, :], vb, preferred_element_type=F32))

    def step(t, slot):
        qk(t + 1, 1 - slot, full)
        pv(jnp.maximum(t - 1, 0), 1 - slot, full)
        softmax(slot, full, False)

    def pair(tt):
        step(2 * tt, 0)
        step(2 * tt + 1, 1)

    qk(0, 0, full)
    one, two = i & 1, i & 2

    @pl.when(one == 1)
    def _():
        pair(0)

    @pl.when(two == 2)
    def _():
        pair(one)
        pair(one + 1)

    def four_pairs(n, carry):
        for j in range(4):
            pair(one + two + 4 * n + j)
        return carry

    lax.fori_loop(0, i // 4, four_pairs, 0)
    d = 2 * i
    qk(d + 1, 1, hi)
    pv(jnp.maximum(d - 1, 0), 1, full)
    softmax(0, lo, True)
    softmax(0, hi, False)
    softmax(1, hi, True)
    pv(d, 0, full)
    pv(d + 1, 1, hi)
    g = g_ref[...].astype(F32)
    acc = acc_ref[...]
    o_ref[...] = (acc[:, :V_DIM] / acc[:, V_DIM:] * (g * jax.nn.sigmoid(g))).astype(o_ref.dtype)


def _attention(q, k, v_aug, u, seq, tq, sub):
    sub = min(sub, tq // 2)
    assert (tq // 2) % sub == 0 and seq % tq == 0
    batch = q.shape[0]
    t = batch * seq
    nq = seq // tq
    return pl.pallas_call(
        functools.partial(_attn_kernel, tq=tq, sub=sub),
        grid=(batch, MLA_HEADS, nq),
        in_specs=[pl.BlockSpec((None, None, tq, QK_HEAD), lambda b, h, i: (b, h, i, 0)),
                  pl.BlockSpec((None, None, seq, QK_HEAD), lambda b, h, i: (b, h, 0, 0)),
                  pl.BlockSpec((None, None, seq, 2 * V_DIM), lambda b, h, i: (b, h, 0, 0)),
                  pl.BlockSpec((tq, V_DIM), lambda b, h, i: (b * nq + i, U_MG // V_DIM + h))],
        out_specs=pl.BlockSpec((tq, V_DIM), lambda b, h, i: (b * nq + i, h)),
        out_shape=jax.ShapeDtypeStruct((t, MLA_DIM), BF16),
        scratch_shapes=[pltpu.VMEM((tq, LANES), F32), pltpu.VMEM((tq, 2 * V_DIM), F32),
                        pltpu.VMEM((tq, tq // 2), F32), pltpu.VMEM((tq, tq // 2), F32),
                        pltpu.VMEM((tq, tq // 2), BF16), pltpu.VMEM((tq, tq // 2), BF16),
                        pltpu.VMEM((tq, LANES), F32), pltpu.VMEM((tq, LANES), F32),
                        pltpu.VMEM((tq, LANES), F32), pltpu.VMEM((tq, LANES), F32)],
        compiler_params=_cparams(("parallel", "parallel", "arbitrary")),
        name="mla_attention",
    )(q, k, v_aug, u)


DECAY_SCALE = 0.6065306597126334


def _rwkv_prep_kernel(r_ref, k_ref, v_ref, wa_ref, rp_ref, kp_ref, vp_ref, wap_ref,
                      mu_r, mu_k, mu_v, mu_wa, w0, w2p, a0, a2p, kkw, kaw, eones,
                      r_o, lw_o, k_o, v_o, a_o, b_o, *, tiles_per_seq):
    first = (pl.program_id(0) % tiles_per_seq) == 0
    row = lax.broadcasted_iota(jnp.int32, r_ref.shape, 0)
    row_wa = lax.broadcasted_iota(jnp.int32, wa_ref.shape, 0)

    def lerp(x_ref, p_ref, mu, rw):
        x = x_ref[...].astype(F32)
        prev = jnp.where(first, 0.0, p_ref[...].astype(F32))
        return x + (_shift_rows(x, prev, 1, rw) - x) * mu[...]

    r = lerp(r_ref, rp_ref, mu_r, row)
    k = lerp(k_ref, kp_ref, mu_k, row)
    v = lerp(v_ref, vp_ref, mu_v, row)
    wa = lerp(wa_ref, wap_ref, mu_wa, row_wa)
    z = w0[...] + _bdot(jnp.tanh(wa), w2p[...])
    a_sig = jax.nn.sigmoid(a0[...] + _bdot(wa, a2p[...]))
    kk = k * kkw[...]
    ss = _bdot(kk * kk, eones[...])
    kk = kk * jnp.minimum(lax.rsqrt(ss), 1e12)
    r_o[...] = r
    lw_o[...] = -DECAY_SCALE * jax.nn.sigmoid(z)
    k_o[...] = k * (1.0 + (a_sig - 1.0) * kaw[...])
    v_o[...] = v
    a_o[...] = -kk
    b_o[...] = kk * a_sig


def _rwkv_prep(u, mu_r, mu_k, mu_v, mu_wa, w0, w2p, a0, a2p, kkw, kaw, eones, seq, tm):
    t = u.shape[0]
    cblk = lambda c: pl.BlockSpec((tm, RWKV_DIM), lambda i, c=c: (i, c))
    prow = lambda i: jnp.maximum(i * (tm // PREV_ROWS) - 1, 0)
    pblk = lambda c: pl.BlockSpec((PREV_ROWS, RWKV_DIM), lambda i, c=c: (prow(i), c))
    full = lambda a: pl.BlockSpec(a.shape, lambda i: (0,) * a.ndim)
    out = jax.ShapeDtypeStruct((t, RWKV_DIM), F32)
    oblk = pl.BlockSpec((tm, RWKV_DIM), lambda i: (i, 0))
    return pl.pallas_call(
        functools.partial(_rwkv_prep_kernel, tiles_per_seq=seq // tm),
        grid=(t // tm,),
        in_specs=[cblk(U_R // 512), cblk(U_K // 512), cblk(U_V // 512),
                  pl.BlockSpec((tm, LANES), lambda i: (i, U_WA // LANES)),
                  pblk(U_R // 512), pblk(U_K // 512), pblk(U_V // 512),
                  pl.BlockSpec((PREV_ROWS, LANES), lambda i: (prow(i), U_WA // LANES)),
                  full(mu_r), full(mu_k), full(mu_v), full(mu_wa), full(w0), full(w2p),
                  full(a0), full(a2p), full(kkw), full(kaw), full(eones)],
        out_specs=[oblk] * 6,
        out_shape=[out] * 6,
        compiler_params=_cparams(("parallel",)),
        name="rwkv_prep",
    )(u, u, u, u, u, u, u, u, mu_r, mu_k, mu_v, mu_wa, w0, w2p, a0, a2p, kkw, kaw, eones)


def _stack(x, even_lane):
    return jnp.concatenate([jnp.where(even_lane, x, 0.0), jnp.where(even_lane, 0.0, x)], axis=0)


def _unstack(x):
    return x[:CHUNK] + x[CHUNK:]


def _each(fn, *lists):
    return [fn(*args) for args in zip(*lists)]


def _chunk_terms(r, lw, k, v, a, b, masks):
    even_lane, strict, incl, same_blk, eye, tri = masks
    nt = (((1,), (1,)), ((), ()))
    tn = (((0,), (0,)), ((), ()))

    def cumsum(x):
        hi = x.astype(BF16)
        lo = (x - hi.astype(F32)).astype(BF16)
        return jnp.dot(tri, jnp.concatenate([hi, lo], axis=0), preferred_element_type=F32)

    cum = _each(cumsum, lw)
    cum_end = _each(lambda c: c[CHUNK - 1:CHUNK], cum)
    at_s = _each(lambda x, c, w: _stack(x * jnp.exp(c - w), even_lane), a, cum, lw)
    rt_s = _each(lambda x, c: _stack(x * jnp.exp(c), even_lane), r, cum)
    v_s = _each(lambda x: _stack(x, even_lane), v)
    bt = _each(lambda x, c: (x * jnp.exp(-c)).astype(BF16), b, cum)
    kt = _each(lambda x, c: (x * jnp.exp(-c)).astype(BF16), k, cum)
    g = _each(lambda x, y, p, q: lax.dot_general(
        jnp.concatenate([x, y], axis=0).astype(BF16), jnp.concatenate([p, p, q, q], axis=0), nt,
        preferred_element_type=F32), at_s, rt_s, bt, kt)
    x = _each(lambda m: jnp.where(strict, m[:LANES, :LANES], 0.0), g)
    a_ak = _each(lambda m: jnp.where(strict, m[:LANES, LANES:], 0.0), g)
    a_r = _each(lambda m: jnp.concatenate([jnp.where(incl, m[LANES:, :LANES], 0.0),
                                           jnp.where(incl, m[LANES:, LANES:], 0.0)], axis=1), g)
    akv = _each(_bdot, a_ak, v_s)

    tinv = _each(lambda m: jnp.where(eye, 1.0, 0.0) + m, x)
    n = 2
    while n < CHUNK:
        x = _each(lambda m: _bdot(m, m), x)
        tinv = _each(lambda t, m: t + _bdot(t, m), tinv, x)
        n *= 2

    wu = _each(lambda t, p, q: _bdot(t, jnp.concatenate([p, q], axis=1)), tinv, at_s, akv)
    zeros = jnp.zeros((LANES, LANES), F32)
    top = _each(lambda m, w, q: _bdot(m, jnp.concatenate(
        [w, jnp.concatenate([zeros, q], axis=1)], axis=0)), a_r, wu, v_s)
    rw_s = _each(lambda x, t: x + t[:, :LANES], rt_s, top)
    y0_s = _each(lambda t: t[:, LANES:], top)

    def carry_terms(bb, kk, vv, c, ce, w):
        to_end = jnp.exp(ce - c)
        bk = jnp.concatenate([bb * to_end, kk * to_end], axis=0)
        rhs = jnp.concatenate(
            [jnp.concatenate([_unstack(w[:, :LANES]), _unstack(w[:, LANES:])], axis=1),
             jnp.concatenate([jnp.zeros((CHUNK, LANES), F32), vv], axis=1)], axis=0)
        mz = lax.dot_general(bk.astype(BF16), rhs.astype(BF16), tn, preferred_element_type=F32)
        m = jnp.where(same_blk, mz[:, :LANES], 0.0) + jnp.where(eye, jnp.exp(ce), 0.0)
        return m, jnp.where(same_blk, mz[:, LANES:], 0.0)

    mz = _each(carry_terms, b, k, v, cum, cum_end, wu)
    return rw_s, y0_s, [t[0] for t in mz], [t[1] for t in mz]


def _rwkv_scan_kernel(r_ref, lw_ref, k_ref, v_ref, a_ref, b_ref, g_ref, gng, gnb, rkw, eones,
                      o_ref, state_ref, y_ref, *, tt):
    @pl.when(pl.program_id(0) == 0)
    def _():
        state_ref[...] = jnp.zeros(state_ref.shape, F32)

    rr = lax.broadcasted_iota(jnp.int32, (LANES, LANES), 0)
    cc = lax.broadcasted_iota(jnp.int32, (LANES, LANES), 1)
    same_blk = (rr // CHUNK) == (cc // CHUNK)
    strict = same_blk & ((cc % CHUNK) < (rr % CHUNK))
    incl = same_blk & ((cc % CHUNK) <= (rr % CHUNK))
    eye = rr == cc
    even_lane = lax.broadcasted_iota(jnp.int32, (CHUNK, LANES), 1) < CHUNK
    tr = lax.broadcasted_iota(jnp.int32, (CHUNK, CHUNK), 0)
    tc = lax.broadcasted_iota(jnp.int32, (CHUNK, CHUNK), 1)
    tri = jnp.where(tc <= tr, 1.0, 0.0).astype(BF16)
    tri = jnp.concatenate([tri, tri], axis=1)
    masks = (even_lane, strict, incl, same_blk, eye, tri)

    batch = r_ref.shape[0]
    chains = [(bi, slice(p * LANES, (p + 1) * LANES))
              for bi in range(batch) for p in range(RWKV_DIM // LANES)]

    nch = len(chains)

    def chunk_body(c, carry):
        rows = [pl.ds(pl.multiple_of((c * UNROLL + j) * CHUNK, CHUNK), CHUNK) for j in range(UNROLL)]
        load = lambda ref: [ref[bi, rw, cs] for rw in rows for bi, cs in chains]
        rw_s, y0_s, m, z0 = _chunk_terms(load(r_ref), load(lw_ref), load(k_ref), load(v_ref),
                                         load(a_ref), load(b_ref), masks)
        st = [state_ref[n] for n in range(nch)]
        for j in range(UNROLL):
            ys = [_bdot(jnp.concatenate([_unstack(rw_s[j * nch + n]), m[j * nch + n]], axis=0), st[n])
                  for n in range(nch)]
            for n, (bi, cs) in enumerate(chains):
                y_ref[bi, rows[j], cs] = ys[n][:CHUNK] + _unstack(y0_s[j * nch + n])
            st = [ys[n][CHUNK:] + z0[j * nch + n] for n in range(nch)]
        for n in range(nch):
            state_ref[n] = st[n]
        return carry

    lax.fori_loop(0, tt // (CHUNK * UNROLL), chunk_body, 0)

    e = eones[...]

    def headsum(x):
        return jnp.dot(x.astype(BF16), e, preferred_element_type=F32)

    for bi in range(batch):
        y = y_ref[bi]
        d = y - headsum(y) * (1.0 / RWKV_N)
        var = headsum(d * d) * (1.0 / RWKV_N)
        yn = d * lax.rsqrt(var + RWKV_GN_EPS) * gng[...] + gnb[...]
        yn = yn + headsum(r_ref[bi] * k_ref[bi] * rkw[...]) * v_ref[bi]
        g = g_ref[bi].astype(F32)
        o_ref[bi] = (yn * (g * jax.nn.sigmoid(g))).astype(o_ref.dtype)


def _rwkv_scan(r, lw, k, v, a, b, u, gng, gnb, rkw, eones, batch, seq, tt):
    to3 = lambda x: x.reshape(batch, seq, x.shape[-1])
    blk = pl.BlockSpec((batch, tt, RWKV_DIM), lambda i: (0, i, 0))
    full = lambda x: pl.BlockSpec(x.shape, lambda i: (0,) * x.ndim)
    out = pl.pallas_call(
        functools.partial(_rwkv_scan_kernel, tt=tt),
        grid=(seq // tt,),
        in_specs=[blk] * 6 + [pl.BlockSpec((batch, tt, RWKV_DIM), lambda i: (0, i, U_RG // RWKV_DIM)),
                              full(gng), full(gnb), full(rkw), full(eones)],
        out_specs=blk,
        out_shape=jax.ShapeDtypeStruct((batch, seq, RWKV_DIM), BF16),
        scratch_shapes=[pltpu.VMEM((batch * RWKV_DIM // LANES, LANES, LANES), F32),
                        pltpu.VMEM((batch, tt, RWKV_DIM), F32)],
        compiler_params=_cparams(("arbitrary",)),
        name="rwkv_scan",
    )(to3(r), to3(lw), to3(k), to3(v), to3(a), to3(b), to3(u), gng, gnb, rkw, eones)
    return out.reshape(batch * seq, RWKV_DIM)


def _outproj_kernel(cb, cc, ch, cg, ccp, chp, cw, ym, yr, x_ref, w1, w2, w3, lg, lb, o_ref, yc,
                    *, tiles_per_seq):
    first = (pl.program_id(0) % tiles_per_seq) == 0
    yc[...] = _conv_tile(cb, cc, ch, cg, ccp, chp, cw, first)
    sub = min(256, o_ref.shape[0])
    for r0 in range(0, o_ref.shape[0], sub):
        rs = slice(r0, r0 + sub)
        acc = jnp.dot(yc[rs, :], w1[...], preferred_element_type=F32)
        acc = acc + jnp.dot(ym[rs, :], w2[...], preferred_element_type=F32)
        acc = acc + jnp.dot(yr[rs, :], w3[...], preferred_element_type=F32)
        z = DEEPNORM_ALPHA * x_ref[rs, :] + acc
        mu = jnp.mean(z, -1, keepdims=True)
        d = z - mu
        var = jnp.mean(d * d, -1, keepdims=True)
        o_ref[rs, :] = d * lax.rsqrt(var + LN_EPS) * lg[...] + lb[...]


def _outproj(u, conv_w, ym, yr, x2, w1, w2, w3, lg, lb, seq, tm):
    t = x2.shape[0]
    rblk = lambda d: pl.BlockSpec((tm, d), lambda i: (i, 0))
    full = lambda a: pl.BlockSpec(a.shape, lambda i: (0,) * a.ndim)
    cblk = lambda c: pl.BlockSpec((tm, CONV_DIM), lambda i, c=c: (i, c))
    pblk = lambda c: pl.BlockSpec((PREV_ROWS, CONV_DIM),
                                  lambda i, c=c: (jnp.maximum(i * (tm // PREV_ROWS) - 1, 0), c))
    return pl.pallas_call(
        functools.partial(_outproj_kernel, tiles_per_seq=seq // tm),
        grid=(t // tm,),
        in_specs=[cblk(U_CB // 512), cblk(U_CC // 512), cblk(U_CH // 512), cblk(U_CG // 512),
                  pblk(U_CC // 512), pblk(U_CH // 512), full(conv_w),
                  rblk(MLA_DIM), rblk(RWKV_DIM), rblk(D_MODEL),
                  full(w1), full(w2), full(w3), full(lg), full(lb)],
        out_specs=rblk(D_MODEL),
        out_shape=jax.ShapeDtypeStruct((t, D_MODEL), F32),
        scratch_shapes=[pltpu.VMEM((tm, CONV_DIM), BF16)],
        compiler_params=_cparams(("parallel",)),
        name="outproj_ln",
    )(u, u, u, u, u, u, conv_w, ym, yr, x2, w1, w2, w3, lg, lb)


def _rot_half_cols(w):
    return jnp.concatenate([-w[:, QK_ROPE // 2:], w[:, :QK_ROPE // 2]], axis=1)


def _pack_w_in(w):
    o_ckv, o_kpe, o_mg, o_rc = 2560, 2816, 2880, 3904
    ckv = w[:, o_ckv:o_kpe]
    kpe = w[:, o_kpe:o_mg]
    mg = w[:, o_mg:o_rc]
    r = w[:, o_rc:o_rc + 512]
    wd = w[:, o_rc + 512:o_rc + 576]
    k = w[:, o_rc + 576:o_rc + 1088]
    v = w[:, o_rc + 1088:o_rc + 1600]
    ad = w[:, o_rc + 1600:o_rc + 1664]
    rg = w[:, o_rc + 1664:]
    return jnp.concatenate([w[:, :o_ckv], r, k, v, rg, mg, ckv, kpe, _rot_half_cols(kpe),
                            wd, ad], axis=1).astype(BF16)


def _pack_mu(mu):
    r, wd, k, v, ad = (mu[0:512], mu[512:576], mu[576:1088], mu[1088:1600], mu[1600:1664])
    row = lambda a: a.reshape(1, -1)
    return row(r), row(k), row(v), row(jnp.concatenate([wd, ad]))


def _pack_w_uq(w):
    cols = []
    for h in range(MLA_HEADS):
        wh = w[:, h * QK_HEAD:(h + 1) * QK_HEAD]
        pe = wh[:, QK_NOPE:]
        cols += [wh[:, :QK_NOPE], pe, _rot_half_cols(pe)]
    return jnp.concatenate(cols, axis=1).astype(BF16)


def _tile(n, pref):
    return pref if n % pref == 0 else n


def kernel(x, positions, w_in, conv_w, q_norm_g, w_uq, kv_norm_g, w_ukv, rwkv_mu, rwkv_w0,
           rwkv_w2, rwkv_a0, rwkv_a2, rwkv_k_k, rwkv_k_a, rwkv_r_k, rwkv_gn_g, rwkv_gn_b,
           w_out, ln_g, ln_b):
    batch, seq, _ = x.shape
    t = batch * seq
    row = lambda a: a.reshape(1, -1)

    inv_freq = ROPE_THETA ** (-jnp.arange(0, QK_ROPE, 2, dtype=F32) / QK_ROPE)
    invf4 = jnp.tile(inv_freq, 4).reshape(1, LANES)
    cs = _rope_table(positions.reshape(t, 1), invf4, _tile(t, TM_PREP))

    head_of = jnp.arange(RWKV_DIM) // RWKV_N
    eones = (head_of[:, None] == head_of[None, :]).astype(BF16)
    zpad = jnp.zeros((DECAY_LORA, RWKV_DIM), F32)

    x2 = x.reshape(t, D_MODEL)
    for l in range(DEPTH):
        u = _inproj(x2, _pack_w_in(w_in[l]), _tile(t, TM_PROJ), TN_PROJ)
        q, k, v = _mla_prep(u, cs, row(q_norm_g[l]), _pack_w_uq(w_uq[l]), row(kv_norm_g[l]),
                            w_ukv[l].astype(BF16), batch, seq, _tile(seq, TM_PREP))
        y_mla = _attention(q, k, v, u, seq, _tile(seq, TQ_ATTN), SUB_ATTN)
        mu_r, mu_k, mu_v, mu_wa = _pack_mu(rwkv_mu[l])
        w2p = jnp.concatenate([rwkv_w2[l], zpad], axis=0).astype(BF16)
        a2p = jnp.concatenate([zpad, rwkv_a2[l]], axis=0).astype(BF16)
        rr, lw, kk, vv, aa, bb = _rwkv_prep(
            u, mu_r, mu_k, mu_v, mu_wa, row(rwkv_w0[l]), w2p, row(rwkv_a0[l]), a2p,
            row(rwkv_k_k[l]), row(rwkv_k_a[l]), eones, seq, _tile(seq, TM_PREP))
        y_rwkv = _rwkv_scan(rr, lw, kk, vv, aa, bb, u, row(rwkv_gn_g[l]), row(rwkv_gn_b[l]),
                            row(rwkv_r_k[l]), eones, batch, seq, _tile(seq, TT_SCAN))
        wo = w_out[l].astype(BF16)
        x2 = _outproj(u, conv_w[l], y_mla, y_rwkv, x2,
                      wo[:CONV_DIM], wo[CONV_DIM:CONV_DIM + MLA_DIM], wo[CONV_DIM + MLA_DIM:],
                      row(ln_g[l]), row(ln_b[l]), seq, _tile(seq, TM_OUT))
    return x2.reshape(batch, seq, D_MODEL)
```

```python
import functools

import jax
import jax.numpy as jnp
from jax import lax
from jax.experimental import pallas as pl
from jax.experimental.pallas import tpu as pltpu

F32 = jnp.float32
BF16 = jnp.bfloat16

D_MODEL = 2048
CONV_DIM = 512
CONV_K = 3
MLA_HEADS = 8
QK_NOPE = 128
QK_ROPE = 64
QK_HEAD = QK_NOPE + QK_ROPE
V_DIM = 128
MLA_DIM = MLA_HEADS * V_DIM
Q_LORA = 512
KV_LORA = 256
ROPE_THETA = 10000.0
RWKV_HEADS = 8
RWKV_N = 64
RWKV_DIM = RWKV_HEADS * RWKV_N
DECAY_LORA = 64
A_LORA = 64
RWKV_GN_EPS = 64e-5
LN_EPS = 1e-5
RMS_EPS = 1e-6
DEPTH = 2
DEEPNORM_ALPHA = (2 * DEPTH) ** 0.25

U_CB, U_CC, U_CH, U_CG = 0, 512, 1024, 1536
U_CQ = 2048
U_R, U_K, U_V, U_RG = 2560, 3072, 3584, 4096
U_MG = 4608
U_CKV = 5632
U_KPE = 5888
U_WA = 6016
U_TOTAL = 6144

LANES = 128
PREV_ROWS = 16
CHUNK = 64
UNROLL = 2
VMEM_LIMIT = 56 * 1024 * 1024

TM_PROJ = 1024
TN_PROJ = 2048
TM_PREP = 1024
TQ_ATTN = 1024
SUB_ATTN = 512
TT_SCAN = 512
TM_OUT = 512

LOG2E = 1.4426950408889634


def _cparams(sem):
    return pltpu.CompilerParams(dimension_semantics=sem, vmem_limit_bytes=VMEM_LIMIT)


def _bdot(a, b):
    return jnp.dot(a.astype(BF16), b.astype(BF16), preferred_element_type=F32)


def _inproj_kernel(x_ref, w_ref, o_ref, xb_ref):
    @pl.when(pl.program_id(1) == 0)
    def _():
        xb_ref[...] = x_ref[...].astype(BF16)

    o_ref[...] = jnp.dot(xb_ref[...], w_ref[...], preferred_element_type=F32).astype(o_ref.dtype)


def _inproj(x2, w_p, layer, tm, tn):
    t = x2.shape[0]
    return pl.pallas_call(
        _inproj_kernel,
        grid=(t // tm, U_TOTAL // tn),
        in_specs=[pl.BlockSpec((tm, D_MODEL), lambda i, j: (i, 0)),
                  pl.BlockSpec((None, D_MODEL, tn), lambda i, j: (layer, 0, j))],
        out_specs=pl.BlockSpec((tm, tn), lambda i, j: (i, j)),
        out_shape=jax.ShapeDtypeStruct((t, U_TOTAL), BF16),
        scratch_shapes=[pltpu.VMEM((tm, D_MODEL), BF16)],
        compiler_params=_cparams(("parallel", "arbitrary")),
        name="inproj",
    )(x2, w_p)


def _shift_rows(x, prev, n, row):
    out = pltpu.roll(x, n, 0)
    for r in range(n):
        out = jnp.where(row == r, prev[PREV_ROWS - n + r:PREV_ROWS - n + r + 1], out)
    return out


def _conv_tile(cb, cc, ch, cg, ccp, chp, w, first):
    up = cc[...].astype(F32) * ch[...].astype(F32)
    prev = jnp.where(first, 0.0, ccp[...].astype(F32) * chp[...].astype(F32))
    row = lax.broadcasted_iota(jnp.int32, up.shape, 0)
    u1 = _shift_rows(up, prev, 1, row)
    u2 = _shift_rows(up, prev, 2, row)
    wv = w[...]
    y = wv[0:1] * u2 + wv[1:2] * u1 + wv[2:3] * up
    g = cg[...].astype(F32)
    return (cb[...].astype(F32) * y * (g * jax.nn.sigmoid(g))).astype(BF16)


def _rope_kernel(pos_ref, invf_ref, cs_ref):
    ang = pos_ref[...].astype(F32) * invf_ref[...]
    lane = lax.broadcasted_iota(jnp.int32, ang.shape, 1)
    cs_ref[...] = jnp.where(lane < QK_ROPE, jnp.cos(ang), jnp.sin(ang))


def _rope_table(pos2, invf4, tm):
    t = pos2.shape[0]
    return pl.pallas_call(
        _rope_kernel,
        grid=(t // tm,),
        in_specs=[pl.BlockSpec((tm, 1), lambda i: (i, 0)),
                  pl.BlockSpec((1, LANES), lambda i: (0, 0))],
        out_specs=pl.BlockSpec((tm, LANES), lambda i: (i, 0)),
        out_shape=jax.ShapeDtypeStruct((t, LANES), F32),
        compiler_params=_cparams(("parallel",)),
        name="rope_table",
    )(pos2, invf4)


def _rms(x, g):
    return x * lax.rsqrt(jnp.mean(x * x, -1, keepdims=True) + RMS_EPS) * g


def _mla_prep_kernel(cq_ref, ckv_ref, kp_ref, cs_ref, qg_ref, wq_ref, kg_ref, wkv_ref,
                     q_out, k_out, v_out, *, qscale):
    cs = cs_ref[...]
    qf = _bdot(_rms(cq_ref[...].astype(F32), qg_ref[...]), wq_ref[...])
    kvf = _bdot(_rms(ckv_ref[...].astype(F32), kg_ref[...]), wkv_ref[...])
    kp = kp_ref[...].astype(F32) * cs
    kpe = (kp + pltpu.roll(kp, QK_ROPE, 1))[:, :QK_ROPE].astype(BF16)
    for h in range(MLA_HEADS):
        c0 = 2 * LANES * h
        pp = qf[:, c0 + LANES:c0 + 2 * LANES] * cs
        pe = pp + pltpu.roll(pp, QK_ROPE, 1)
        q_out[h, :, 0:QK_NOPE] = (qf[:, c0:c0 + LANES] * qscale).astype(BF16)
        q_out[h, :, QK_NOPE:QK_HEAD] = (pe[:, :QK_ROPE] * qscale).astype(BF16)
        k_out[h, :, 0:QK_NOPE] = kvf[:, c0:c0 + LANES].astype(BF16)
        k_out[h, :, QK_NOPE:QK_HEAD] = kpe
        v_out[h, :, 0:V_DIM] = kvf[:, c0 + LANES:c0 + 2 * LANES].astype(BF16)
        v_out[h, :, V_DIM:2 * V_DIM] = jnp.ones((kvf.shape[0], V_DIM), BF16)


def _mla_prep(u, cs, qg, wq_p, kg, wkv, batch, seq, tm):
    t = u.shape[0]
    nt = seq // tm
    hblk = lambda d: pl.BlockSpec((None, MLA_HEADS, tm, d), lambda i: (i // nt, 0, i % nt, 0))
    full = lambda a: pl.BlockSpec(a.shape, lambda i: (0,) * a.ndim)
    qscale = QK_HEAD ** -0.5 * LOG2E
    return pl.pallas_call(
        functools.partial(_mla_prep_kernel, qscale=qscale),
        grid=(t // tm,),
        in_specs=[pl.BlockSpec((tm, Q_LORA), lambda i: (i, U_CQ // Q_LORA)),
                  pl.BlockSpec((tm, KV_LORA), lambda i: (i, U_CKV // KV_LORA)),
                  pl.BlockSpec((tm, LANES), lambda i: (i, U_KPE // LANES)),
                  pl.BlockSpec((tm, LANES), lambda i: (i, 0)),
                  full(qg), full(wq_p), full(kg), full(wkv)],
        out_specs=[hblk(QK_HEAD), hblk(QK_HEAD), hblk(2 * V_DIM)],
        out_shape=[jax.ShapeDtypeStruct((batch, MLA_HEADS, seq, QK_HEAD), BF16),
                   jax.ShapeDtypeStruct((batch, MLA_HEADS, seq, QK_HEAD), BF16),
                   jax.ShapeDtypeStruct((batch, MLA_HEADS, seq, 2 * V_DIM), BF16)],
        compiler_params=_cparams(("parallel",)),
        name="mla_prep",
    )(u, u, u, cs, qg, wq_p, kg, wkv)


def _attn_kernel(q_ref, k_ref, v_ref, g_ref, o_ref, m_ref, acc_ref, s0, s1, p0, p1, a0, a1, x0, x1, *, tq, sub):
    tk = tq // 2
    i = pl.program_id(2)
    s_scr, p_scr, a_scr, x_scr = (s0, s1), (p0, p1), (a0, a1), (x0, x1)
    lo, hi, full = slice(0, tk), slice(tk, tq), slice(0, tq)
    nt = (((1,), (1,)), ((), ()))

    m_ref[...] = jnp.full(m_ref.shape, -jnp.inf, F32)
    acc_ref[...] = jnp.zeros(acc_ref.shape, F32)
    p1[...] = jnp.zeros(p1.shape, BF16)
    a1[...] = jnp.ones(a1.shape, F32)

    def qk(t, slot, rows):
        kb = k_ref[pl.ds(pl.multiple_of(t * tk, tk), tk), :]
        for r0 in range(rows.start, rows.stop, sub):
            rs = slice(r0, r0 + sub)
            s = lax.dot_general(q_ref[rs, :], kb, nt, preferred_element_type=F32)
            s_scr[slot][rs, :] = s
            x_scr[slot][rs, :] = jnp.broadcast_to(jnp.max(s, axis=1, keepdims=True), (sub, LANES))

    def softmax(slot, rows, causal):
        for r0 in range(rows.start, rows.stop, sub):
            rs = slice(r0, r0 + sub)
            s = s_scr[slot][rs, :]
            if causal:
                r = lax.broadcasted_iota(jnp.int32, s.shape, 0) + (r0 - rows.start)
                c = lax.broadcasted_iota(jnp.int32, s.shape, 1)
                s = jnp.where(c <= r, s, -jnp.inf)
                s_max = jnp.max(s, axis=1, keepdims=True)
            else:
                s_max = x_scr[slot][rs, :]
            m_prev = m_ref[rs, :]
            m_new = jnp.maximum(m_prev, s_max)
            p_scr[slot][rs, :] = jnp.exp2(s - jnp.tile(m_new, (1, tk // LANES))).astype(BF16)
            a_scr[slot][rs, :] = jnp.exp2(m_prev - m_new)
            m_ref[rs, :] = m_new

    def pv(t, slot, rows):
        vb = v_ref[pl.ds(pl.multiple_of(t * tk, tk), tk), :]
        acc_ref[rows, :] = (jnp.tile(a_scr[slot][rows, :], (1, 2)) * acc_ref[rows, :]
                            + jnp.dot(p_scr[slot][rows, :], vb, preferred_element_type=F32))

    def step(t, slot):
        qk(t + 1, 1 - slot, full)
        pv(jnp.maximum(t - 1, 0), 1 - slot, full)
        softmax(slot, full, False)

    def pair(tt):
        step(2 * tt, 0)
        step(2 * tt + 1, 1)

    qk(0, 0, full)
    one, two = i & 1, i & 2

    @pl.when(one == 1)
    def _():
        pair(0)

    @pl.when(two == 2)
    def _():
        pair(one)
        pair(one + 1)

    def four_pairs(n, carry):
        for j in range(4):
            pair(one + two + 4 * n + j)
        return carry

    lax.fori_loop(0, i // 4, four_pairs, 0)
    d = 2 * i
    qk(d + 1, 1, hi)
    pv(jnp.maximum(d - 1, 0), 1, full)
    softmax(0, lo, True)
    softmax(0, hi, False)
    softmax(1, hi, True)
    pv(d, 0, full)
    pv(d + 1, 1, hi)
    g = g_ref[...].astype(F32)
    acc = acc_ref[...]
    o_ref[...] = (acc[:, :V_DIM] / acc[:, V_DIM:] * (g * jax.nn.sigmoid(g))).astype(o_ref.dtype)


def _attention(q, k, v_aug, u, seq, tq, sub):
    sub = min(sub, tq // 2)
    assert (tq // 2) % sub == 0 and seq % tq == 0
    batch = q.shape[0]
    t = batch * seq
    nq = seq // tq
    return pl.pallas_call(
        functools.partial(_attn_kernel, tq=tq, sub=sub),
        grid=(batch, MLA_HEADS, nq),
        in_specs=[pl.BlockSpec((None, None, tq, QK_HEAD), lambda b, h, i: (b, h, i, 0)),
                  pl.BlockSpec((None, None, seq, QK_HEAD), lambda b, h, i: (b, h, 0, 0)),
                  pl.BlockSpec((None, None, seq, 2 * V_DIM), lambda b, h, i: (b, h, 0, 0)),
                  pl.BlockSpec((tq, V_DIM), lambda b, h, i: (b * nq + i, U_MG // V_DIM + h))],
        out_specs=pl.BlockSpec((tq, V_DIM), lambda b, h, i: (b * nq + i, h)),
        out_shape=jax.ShapeDtypeStruct((t, MLA_DIM), BF16),
        scratch_shapes=[pltpu.VMEM((tq, LANES), F32), pltpu.VMEM((tq, 2 * V_DIM), F32),
                        pltpu.VMEM((tq, tq // 2), F32), pltpu.VMEM((tq, tq // 2), F32),
                        pltpu.VMEM((tq, tq // 2), BF16), pltpu.VMEM((tq, tq // 2), BF16),
                        pltpu.VMEM((tq, LANES), F32), pltpu.VMEM((tq, LANES), F32),
                        pltpu.VMEM((tq, LANES), F32), pltpu.VMEM((tq, LANES), F32)],
        compiler_params=_cparams(("parallel", "parallel", "arbitrary")),
        name="mla_attention",
    )(q, k, v_aug, u)


DECAY_SCALE = 0.6065306597126334


def _rwkv_prep_kernel(r_ref, k_ref, v_ref, wa_ref, rp_ref, kp_ref, vp_ref, wap_ref,
                      mu_r, mu_k, mu_v, mu_wa, w0, w2p, a0, a2p, kkw, kaw, eones,
                      r_o, lw_o, k_o, v_o, a_o, b_o, *, tiles_per_seq):
    first = (pl.program_id(0) % tiles_per_seq) == 0
    row = lax.broadcasted_iota(jnp.int32, r_ref.shape, 0)
    row_wa = lax.broadcasted_iota(jnp.int32, wa_ref.shape, 0)

    def lerp(x_ref, p_ref, mu, rw):
        x = x_ref[...].astype(F32)
        prev = jnp.where(first, 0.0, p_ref[...].astype(F32))
        return x + (_shift_rows(x, prev, 1, rw) - x) * mu[...]

    r = lerp(r_ref, rp_ref, mu_r, row)
    k = lerp(k_ref, kp_ref, mu_k, row)
    v = lerp(v_ref, vp_ref, mu_v, row)
    wa = lerp(wa_ref, wap_ref, mu_wa, row_wa)
    z = w0[...] + _bdot(jnp.tanh(wa), w2p[...])
    a_sig = jax.nn.sigmoid(a0[...] + _bdot(wa, a2p[...]))
    kk = k * kkw[...]
    ss = _bdot(kk * kk, eones[...])
    kk = kk * jnp.minimum(lax.rsqrt(ss), 1e12)
    r_o[...] = r
    lw_o[...] = -DECAY_SCALE * jax.nn.sigmoid(z)
    k_o[...] = k * (1.0 + (a_sig - 1.0) * kaw[...])
    v_o[...] = v
    a_o[...] = -kk
    b_o[...] = kk * a_sig


def _rwkv_prep(u, mu_r, mu_k, mu_v, mu_wa, w0, w2p, a0, a2p, kkw, kaw, eones, seq, tm):
    t = u.shape[0]
    cblk = lambda c: pl.BlockSpec((tm, RWKV_DIM), lambda i, c=c: (i, c))
    prow = lambda i: jnp.maximum(i * (tm // PREV_ROWS) - 1, 0)
    pblk = lambda c: pl.BlockSpec((PREV_ROWS, RWKV_DIM), lambda i, c=c: (prow(i), c))
    full = lambda a: pl.BlockSpec(a.shape, lambda i: (0,) * a.ndim)
    out = jax.ShapeDtypeStruct((t, RWKV_DIM), F32)
    oblk = pl.BlockSpec((tm, RWKV_DIM), lambda i: (i, 0))
    return pl.pallas_call(
        functools.partial(_rwkv_prep_kernel, tiles_per_seq=seq // tm),
        grid=(t // tm,),
        in_specs=[cblk(U_R // 512), cblk(U_K // 512), cblk(U_V // 512),
                  pl.BlockSpec((tm, LANES), lambda i: (i, U_WA // LANES)),
                  pblk(U_R // 512), pblk(U_K // 512), pblk(U_V // 512),
                  pl.BlockSpec((PREV_ROWS, LANES), lambda i: (prow(i), U_WA // LANES)),
                  full(mu_r), full(mu_k), full(mu_v), full(mu_wa), full(w0), full(w2p),
                  full(a0), full(a2p), full(kkw), full(kaw), full(eones)],
        out_specs=[oblk] * 6,
        out_shape=[out] * 6,
        compiler_params=_cparams(("parallel",)),
        name="rwkv_prep",
    )(u, u, u, u, u, u, u, u, mu_r, mu_k, mu_v, mu_wa, w0, w2p, a0, a2p, kkw, kaw, eones)


def _stack(x, even_lane):
    return jnp.concatenate([jnp.where(even_lane, x, 0.0), jnp.where(even_lane, 0.0, x)], axis=0)


def _unstack(x):
    return x[:CHUNK] + x[CHUNK:]


def _each(fn, *lists):
    return [fn(*args) for args in zip(*lists)]


def _chunk_terms(r, lw, k, v, a, b, masks):
    even_lane, strict, incl, same_blk, eye, tri = masks
    nt = (((1,), (1,)), ((), ()))
    tn = (((0,), (0,)), ((), ()))

    def cumsum(x):
        hi = x.astype(BF16)
        lo = (x - hi.astype(F32)).astype(BF16)
        return jnp.dot(tri, jnp.concatenate([hi, lo], axis=0), preferred_element_type=F32)

    cum = _each(cumsum, lw)
    cum_end = _each(lambda c: c[CHUNK - 1:CHUNK], cum)
    at_s = _each(lambda x, c, w: _stack(x * jnp.exp(c - w), even_lane), a, cum, lw)
    rt_s = _each(lambda x, c: _stack(x * jnp.exp(c), even_lane), r, cum)
    v_s = _each(lambda x: _stack(x, even_lane), v)
    bt = _each(lambda x, c: (x * jnp.exp(-c)).astype(BF16), b, cum)
    kt = _each(lambda x, c: (x * jnp.exp(-c)).astype(BF16), k, cum)
    g = _each(lambda x, y, p, q: lax.dot_general(
        jnp.concatenate([x, y], axis=0).astype(BF16), jnp.concatenate([p, p, q, q], axis=0), nt,
        preferred_element_type=F32), at_s, rt_s, bt, kt)
    x = _each(lambda m: jnp.where(strict, m[:LANES, :LANES], 0.0), g)
    a_ak = _each(lambda m: jnp.where(strict, m[:LANES, LANES:], 0.0), g)
    a_r = _each(lambda m: jnp.concatenate([jnp.where(incl, m[LANES:, :LANES], 0.0),
                                           jnp.where(incl, m[LANES:, LANES:], 0.0)], axis=1), g)
    akv = _each(_bdot, a_ak, v_s)

    tinv = _each(lambda m: jnp.where(eye, 1.0, 0.0) + m, x)
    n = 2
    while n < CHUNK:
        x = _each(lambda m: _bdot(m, m), x)
        tinv = _each(lambda t, m: t + _bdot(t, m), tinv, x)
        n *= 2

    wu = _each(lambda t, p, q: _bdot(t, jnp.concatenate([p, q], axis=1)), tinv, at_s, akv)
    zeros = jnp.zeros((LANES, LANES), F32)
    top = _each(lambda m, w, q: _bdot(m, jnp.concatenate(
        [w, jnp.concatenate([zeros, q], axis=1)], axis=0)), a_r, wu, v_s)
    rw_s = _each(lambda x, t: x + t[:, :LANES], rt_s, top)
    y0_s = _each(lambda t: t[:, LANES:], top)

    def carry_terms(bb, kk, vv, c, ce, w):
        to_end = jnp.exp(ce - c)
        bk = jnp.concatenate([bb * to_end, kk * to_end], axis=0)
        rhs = jnp.concatenate(
            [jnp.concatenate([_unstack(w[:, :LANES]), _unstack(w[:, LANES:])], axis=1),
             jnp.concatenate([jnp.zeros((CHUNK, LANES), F32), vv], axis=1)], axis=0)
        mz = lax.dot_general(bk.astype(BF16), rhs.astype(BF16), tn, preferred_element_type=F32)
        m = jnp.where(same_blk, mz[:, :LANES], 0.0) + jnp.where(eye, jnp.exp(ce), 0.0)
        return m, jnp.where(same_blk, mz[:, LANES:], 0.0)

    mz = _each(carry_terms, b, k, v, cum, cum_end, wu)
    return rw_s, y0_s, [t[0] for t in mz], [t[1] for t in mz]


def _rwkv_scan_kernel(r_ref, lw_ref, k_ref, v_ref, a_ref, b_ref, g_ref, gng, gnb, rkw, eones,
                      o_ref, state_ref, y_ref, *, tt):
    @pl.when(pl.program_id(0) == 0)
    def _():
        state_ref[...] = jnp.zeros(state_ref.shape, F32)

    rr = lax.broadcasted_iota(jnp.int32, (LANES, LANES), 0)
    cc = lax.broadcasted_iota(jnp.int32, (LANES, LANES), 1)
    same_blk = (rr // CHUNK) == (cc // CHUNK)
    strict = same_blk & ((cc % CHUNK) < (rr % CHUNK))
    incl = same_blk & ((cc % CHUNK) <= (rr % CHUNK))
    eye = rr == cc
    even_lane = lax.broadcasted_iota(jnp.int32, (CHUNK, LANES), 1) < CHUNK
    tr = lax.broadcasted_iota(jnp.int32, (CHUNK, CHUNK), 0)
    tc = lax.broadcasted_iota(jnp.int32, (CHUNK, CHUNK), 1)
    tri = jnp.where(tc <= tr, 1.0, 0.0).astype(BF16)
    tri = jnp.concatenate([tri, tri], axis=1)
    masks = (even_lane, strict, incl, same_blk, eye, tri)

    batch = r_ref.shape[0]
    chains = [(bi, slice(p * LANES, (p + 1) * LANES))
              for bi in range(batch) for p in range(RWKV_DIM // LANES)]

    nch = len(chains)

    def chunk_body(c, carry):
        rows = [pl.ds(pl.multiple_of((c * UNROLL + j) * CHUNK, CHUNK), CHUNK) for j in range(UNROLL)]
        load = lambda ref: [ref[bi, rw, cs] for rw in rows for bi, cs in chains]
        rw_s, y0_s, m, z0 = _chunk_terms(load(r_ref), load(lw_ref), load(k_ref), load(v_ref),
                                         load(a_ref), load(b_ref), masks)
        st = [state_ref[n] for n in range(nch)]
        for j in range(UNROLL):
            ys = [_bdot(jnp.concatenate([_unstack(rw_s[j * nch + n]), m[j * nch + n]], axis=0), st[n])
                  for n in range(nch)]
            for n, (bi, cs) in enumerate(chains):
                y_ref[bi, rows[j], cs] = ys[n][:CHUNK] + _unstack(y0_s[j * nch + n])
            st = [ys[n][CHUNK:] + z0[j * nch + n] for n in range(nch)]
        for n in range(nch):
            state_ref[n] = st[n]
        return carry

    lax.fori_loop(0, tt // (CHUNK * UNROLL), chunk_body, 0)

    e = eones[...]

    def headsum(x):
        return jnp.dot(x.astype(BF16), e, preferred_element_type=F32)

    for bi in range(batch):
        y = y_ref[bi]
        d = y - headsum(y) * (1.0 / RWKV_N)
        var = headsum(d * d) * (1.0 / RWKV_N)
        yn = d * lax.rsqrt(var + RWKV_GN_EPS) * gng[...] + gnb[...]
        yn = yn + headsum(r_ref[bi] * k_ref[bi] * rkw[...]) * v_ref[bi]
        g = g_ref[bi].astype(F32)
        o_ref[bi] = (yn * (g * jax.nn.sigmoid(g))).astype(o_ref.dtype)


def _rwkv_scan(r, lw, k, v, a, b, u, gng, gnb, rkw, eones, batch, seq, tt):
    to3 = lambda x: x.reshape(batch, seq, x.shape[-1])
    blk = pl.BlockSpec((batch, tt, RWKV_DIM), lambda i: (0, i, 0))
    full = lambda x: pl.BlockSpec(x.shape, lambda i: (0,) * x.ndim)
    out = pl.pallas_call(
        functools.partial(_rwkv_scan_kernel, tt=tt),
        grid=(seq // tt,),
        in_specs=[blk] * 6 + [pl.BlockSpec((batch, tt, RWKV_DIM), lambda i: (0, i, U_RG // RWKV_DIM)),
                              full(gng), full(gnb), full(rkw), full(eones)],
        out_specs=blk,
        out_shape=jax.ShapeDtypeStruct((batch, seq, RWKV_DIM), BF16),
        scratch_shapes=[pltpu.VMEM((batch * RWKV_DIM // LANES, LANES, LANES), F32),
                        pltpu.VMEM((batch, tt, RWKV_DIM), F32)],
        compiler_params=_cparams(("arbitrary",)),
        name="rwkv_scan",
    )(to3(r), to3(lw), to3(k), to3(v), to3(a), to3(b), to3(u), gng, gnb, rkw, eones)
    return out.reshape(batch * seq, RWKV_DIM)


def _outproj_kernel(cb, cc, ch, cg, ccp, chp, cw, ym, yr, x_ref, w, lg, lb, o_ref, yc,
                    *, tiles_per_seq):
    first = (pl.program_id(0) % tiles_per_seq) == 0
    yc[...] = _conv_tile(cb, cc, ch, cg, ccp, chp, cw, first)
    sub = min(256, o_ref.shape[0])
    for r0 in range(0, o_ref.shape[0], sub):
        rs = slice(r0, r0 + sub)
        acc = jnp.dot(yc[rs, :], w[0:CONV_DIM, :], preferred_element_type=F32)
        acc = acc + jnp.dot(ym[rs, :], w[CONV_DIM:CONV_DIM + MLA_DIM, :], preferred_element_type=F32)
        acc = acc + jnp.dot(yr[rs, :], w[CONV_DIM + MLA_DIM:, :], preferred_element_type=F32)
        z = DEEPNORM_ALPHA * x_ref[rs, :] + acc
        mu = jnp.mean(z, -1, keepdims=True)
        d = z - mu
        var = jnp.mean(d * d, -1, keepdims=True)
        o_ref[rs, :] = d * lax.rsqrt(var + LN_EPS) * lg[...] + lb[...]


def _outproj(u, conv_w, ym, yr, x2, w_all, layer, lg, lb, seq, tm):
    t = x2.shape[0]
    rblk = lambda d: pl.BlockSpec((tm, d), lambda i: (i, 0))
    full = lambda a: pl.BlockSpec(a.shape, lambda i: (0,) * a.ndim)
    cblk = lambda c: pl.BlockSpec((tm, CONV_DIM), lambda i, c=c: (i, c))
    pblk = lambda c: pl.BlockSpec((PREV_ROWS, CONV_DIM),
                                  lambda i, c=c: (jnp.maximum(i * (tm // PREV_ROWS) - 1, 0), c))
    return pl.pallas_call(
        functools.partial(_outproj_kernel, tiles_per_seq=seq // tm),
        grid=(t // tm,),
        in_specs=[cblk(U_CB // 512), cblk(U_CC // 512), cblk(U_CH // 512), cblk(U_CG // 512),
                  pblk(U_CC // 512), pblk(U_CH // 512), full(conv_w),
                  rblk(MLA_DIM), rblk(RWKV_DIM), rblk(D_MODEL),
                  pl.BlockSpec((None,) + w_all.shape[1:], lambda i: (layer, 0, 0)), full(lg), full(lb)],
        out_specs=rblk(D_MODEL),
        out_shape=jax.ShapeDtypeStruct((t, D_MODEL), F32),
        scratch_shapes=[pltpu.VMEM((tm, CONV_DIM), BF16)],
        compiler_params=_cparams(("parallel",)),
        name="outproj_ln",
    )(u, u, u, u, u, u, conv_w, ym, yr, x2, w_all, lg, lb)


def _rot_half_cols(w):
    return jnp.concatenate([-w[:, QK_ROPE // 2:], w[:, :QK_ROPE // 2]], axis=1)


IN_TOTAL = 6080
SRC_KPE = 2816
W_IN_MOVES = (
    (0, 2560, 0),
    (3904, 512, U_R), (4480, 512, U_K), (4992, 512, U_V), (5568, 512, U_RG),
    (2880, 1024, U_MG), (2560, 256, U_CKV), (SRC_KPE, 64, U_KPE),
    (4416, 64, U_WA), (5504, 64, U_WA + 64),
)


def _pack_w_in_kernel(wt_ref, o_ref):
    for src, width, dst in W_IN_MOVES:
        o_ref[:, dst:dst + width] = wt_ref[src:src + width, :].T.astype(BF16)
    half = QK_ROPE // 2
    rot = U_KPE + QK_ROPE
    kpe = wt_ref[SRC_KPE:SRC_KPE + 2 * QK_ROPE, :].T
    swapped = pltpu.roll(kpe, half, 1)
    lane = lax.broadcasted_iota(jnp.int32, kpe.shape, 1)
    rot_half = jnp.where(lane < half, -pltpu.roll(kpe, 2 * QK_ROPE - half, 1), swapped)
    o_ref[:, rot:rot + QK_ROPE] = rot_half[:, :QK_ROPE].astype(BF16)


def _pack_w_in(w_in, tr):
    depth, rows, _ = w_in.shape
    return pl.pallas_call(
        _pack_w_in_kernel,
        grid=(depth, rows // tr),
        in_specs=[pl.BlockSpec((None, IN_TOTAL, tr), lambda l, i: (l, 0, i))],
        out_specs=pl.BlockSpec((None, tr, U_TOTAL), lambda l, i: (l, i, 0)),
        out_shape=jax.ShapeDtypeStruct((depth, rows, U_TOTAL), BF16),
        compiler_params=_cparams(("parallel", "parallel")),
        name="pack_w_in",
    )(jnp.swapaxes(w_in, 1, 2))


def _pack_mu(mu):
    r, wd, k, v, ad = (mu[0:512], mu[512:576], mu[576:1088], mu[1088:1600], mu[1600:1664])
    row = lambda a: a.reshape(1, -1)
    return row(r), row(k), row(v), row(jnp.concatenate([wd, ad]))


def _pack_w_uq(w):
    cols = []
    for h in range(MLA_HEADS):
        wh = w[:, h * QK_HEAD:(h + 1) * QK_HEAD]
        pe = wh[:, QK_NOPE:]
        cols += [wh[:, :QK_NOPE], pe, _rot_half_cols(pe)]
    return jnp.concatenate(cols, axis=1).astype(BF16)


def _tile(n, pref):
    return pref if n % pref == 0 else n


def kernel(x, positions, w_in, conv_w, q_norm_g, w_uq, kv_norm_g, w_ukv, rwkv_mu, rwkv_w0,
           rwkv_w2, rwkv_a0, rwkv_a2, rwkv_k_k, rwkv_k_a, rwkv_r_k, rwkv_gn_g, rwkv_gn_b,
           w_out, ln_g, ln_b):
    batch, seq, _ = x.shape
    t = batch * seq
    row = lambda a: a.reshape(1, -1)

    inv_freq = ROPE_THETA ** (-jnp.arange(0, QK_ROPE, 2, dtype=F32) / QK_ROPE)
    invf4 = jnp.tile(inv_freq, 4).reshape(1, LANES)
    cs = _rope_table(positions.reshape(t, 1), invf4, _tile(t, TM_PREP))

    head_of = jnp.arange(RWKV_DIM) // RWKV_N
    eones = (head_of[:, None] == head_of[None, :]).astype(BF16)
    zpad = jnp.zeros((DECAY_LORA, RWKV_DIM), F32)

    w_in_p = _pack_w_in(w_in, 256)
    w_out_b = w_out.astype(BF16)
    x2 = x.reshape(t, D_MODEL)
    for l in range(DEPTH):
        u = _inproj(x2, w_in_p, l, _tile(t, TM_PROJ), TN_PROJ)
        q, k, v = _mla_prep(u, cs, row(q_norm_g[l]), _pack_w_uq(w_uq[l]), row(kv_norm_g[l]),
                            w_ukv[l].astype(BF16), batch, seq, _tile(seq, TM_PREP))
        y_mla = _attention(q, k, v, u, seq, _tile(seq, TQ_ATTN), SUB_ATTN)
        mu_r, mu_k, mu_v, mu_wa = _pack_mu(rwkv_mu[l])
        w2p = jnp.concatenate([rwkv_w2[l], zpad], axis=0).astype(BF16)
        a2p = jnp.concatenate([zpad, rwkv_a2[l]], axis=0).astype(BF16)
        rr, lw, kk, vv, aa, bb = _rwkv_prep(
            u, mu_r, mu_k, mu_v, mu_wa, row(rwkv_w0[l]), w2p, row(rwkv_a0[l]), a2p,
            row(rwkv_k_k[l]), row(rwkv_k_a[l]), eones, seq, _tile(seq, TM_PREP))
        y_rwkv = _rwkv_scan(rr, lw, kk, vv, aa, bb, u, row(rwkv_gn_g[l]), row(rwkv_gn_b[l]),
                            row(rwkv_r_k[l]), eones, batch, seq, _tile(seq, TT_SCAN))
        x2 = _outproj(u, conv_w[l], y_mla, y_rwkv, x2, w_out_b, l,
                      row(ln_g[l]), row(ln_b[l]), seq, _tile(seq, TM_OUT))
    return x2.reshape(batch, seq, D_MODEL)
```
